```python
import math
import jax, jax.numpy as jnp
from jax import lax
import numpy as np

D_MODEL = 1024
BATCH = 8
SEQ = 2048
DEPTH = 2
DEC_BATCH = 128
DEC_SEQ = 8
PAST_LEN = 16384
PAGE_SIZE = 128

N_META = 16
N_MIXERS = 2
N_CONV_LAYERS = (DEPTH + 1) // 2
N_SSD_LAYERS = DEPTH // 2
CONV_WIDTH = 31
CONV_HIST = CONV_WIDTH - 1
D_CONV = D_MODEL
SSD_EXPAND = 2
D_INNER = SSD_EXPAND * D_MODEL
HEAD_DIM = 64
N_HEADS = D_INNER // HEAD_DIM
N_GROUPS = 8
HEADS_PER_GROUP = N_HEADS // N_GROUPS
D_STATE = 128
SSD_CONV_WIDTH = 4
SSD_CONV_HIST = SSD_CONV_WIDTH - 1
CONV_DIM = D_INNER + 2 * N_GROUPS * D_STATE
IN_PROJ_DIM = D_INNER + CONV_DIM + N_HEADS
CHUNK = 128
D_FF = -(-8 * D_MODEL // (3 * 256)) * 256
EPS = 1e-6

kernel_name = 'conformer_ssd_hybrid_step'


def rmsnorm(x, g):
    xf = x.astype(jnp.float32)
    y = xf * lax.rsqrt(jnp.mean(xf * xf, axis=-1, keepdims=True) + EPS)
    return (y * g.astype(jnp.float32)).astype(x.dtype)


def layernorm(x, g, b):
    xf = x.astype(jnp.float32)
    mu = jnp.mean(xf, axis=-1, keepdims=True)
    xc = xf - mu
    y = xc * lax.rsqrt(jnp.mean(xc * xc, axis=-1, keepdims=True) + EPS)
    return (y * g.astype(jnp.float32) + b.astype(jnp.float32)).astype(x.dtype)


def causal_dwconv(x_hist, w, b):
    c = x_hist.shape[-1]
    y = lax.conv_general_dilated(x_hist, w[:, None, :].astype(x_hist.dtype), window_strides=(1,), padding='VALID',
                                 dimension_numbers=('NWC', 'WIO', 'NWC'), feature_group_count=c)
    return y + b


def swiglu(h, wg, wu, wd):
    return (jax.nn.silu(h @ wg) * (h @ wu)) @ wd


def conformer_conv(h, hist, w_pw1, b_pw1, w_dw, b_dw, ln_g, ln_b, w_pw2, b_pw2):
    u = h @ w_pw1 + b_pw1
    g = u[..., :D_CONV] * jax.nn.sigmoid(u[..., D_CONV:])
    gh = jnp.concatenate([hist.astype(g.dtype), g], axis=1)
    new_hist = gh[:, -CONV_HIST:]
    c = causal_dwconv(gh, w_dw, b_dw)
    c = jax.nn.silu(layernorm(c, ln_g, ln_b))
    return c @ w_pw2 + b_pw2, new_hist


def ssd_scan(x, dt, A, Bm, Cm, h0, chunk):
    b, L = x.shape[:2]
    nc = L // chunk
    x = x.reshape(b, nc, chunk, N_GROUPS, HEADS_PER_GROUP, HEAD_DIM).astype(jnp.float32)
    dt = dt.reshape(b, nc, chunk, N_GROUPS, HEADS_PER_GROUP)
    Bm = Bm.reshape(b, nc, chunk, N_GROUPS, D_STATE).astype(jnp.float32)
    Cm = Cm.reshape(b, nc, chunk, N_GROUPS, D_STATE).astype(jnp.float32)
    acs = jnp.cumsum(dt * A, axis=2)
    diff = acs[:, :, :, None] - acs[:, :, None, :]
    causal = jnp.tril(jnp.ones((chunk, chunk), dtype=bool))[:, :, None, None]
    decay = jnp.exp(jnp.where(causal, diff, -jnp.inf))
    cb = jnp.einsum('bclgn,bcsgn->bclsg', Cm, Bm)
    w = cb[..., None] * decay * dt[:, :, None]
    y_diag = jnp.einsum('bclsgr,bcsgrp->bclgrp', w, x)
    decay_to_end = jnp.exp(acs[:, :, -1:] - acs)
    chunk_states = jnp.einsum('bclgn,bclgr,bclgrp->bcgrpn', Bm, decay_to_end * dt, x)
    chunk_decay = jnp.exp(acs[:, :, -1])

    def step(h, inp):
        s_c, d_c = inp
        return d_c[..., None, None] * h + s_c, h

    h_last, h_prev = lax.scan(step, h0, (jnp.swapaxes(chunk_states, 0, 1), jnp.swapaxes(chunk_decay, 0, 1)))
    h_prev = jnp.swapaxes(h_prev, 0, 1)
    y_off = jnp.einsum('bclgn,bcgrpn,bclgr->bclgrp', Cm, h_prev, jnp.exp(acs))
    y = (y_diag + y_off).reshape(b, L, N_GROUPS, HEADS_PER_GROUP, HEAD_DIM)
    return y, h_last


def ssd_mixer(h, conv_hist, ssm0, n_lead, w_in, conv_w, conv_b, dt_bias, a_log, d_skip, norm_g, w_out):
    bt, L, _ = h.shape
    zxbcdt = h @ w_in
    z = zxbcdt[..., :D_INNER]
    xbc = zxbcdt[..., D_INNER:D_INNER + CONV_DIM]
    dt_raw = zxbcdt[..., D_INNER + CONV_DIM:]
    xh = jnp.concatenate([conv_hist.astype(xbc.dtype), xbc], axis=1)
    new_conv = xh[:, -SSD_CONV_HIST:]
    xbc = jax.nn.silu(causal_dwconv(xh, conv_w, conv_b))
    xs = xbc[..., :D_INNER].reshape(bt, L, N_GROUPS, HEADS_PER_GROUP, HEAD_DIM)
    Bm = xbc[..., D_INNER:D_INNER + N_GROUPS * D_STATE].reshape(bt, L, N_GROUPS, D_STATE)
    Cm = xbc[..., D_INNER + N_GROUPS * D_STATE:].reshape(bt, L, N_GROUPS, D_STATE)
    dt = jax.nn.softplus(dt_raw.astype(jnp.float32) + dt_bias.astype(jnp.float32))
    dt = dt.reshape(bt, L, N_GROUPS, HEADS_PER_GROUP)
    A = -jnp.exp(a_log.astype(jnp.float32)).reshape(N_GROUPS, HEADS_PER_GROUP)
    state = ssm0.astype(jnp.float32).reshape(bt, N_GROUPS, HEADS_PER_GROUP, HEAD_DIM, D_STATE)
    segs = ([(0, n_lead)] if n_lead > 0 else []) + [(n_lead, L)]
    ys = []
    for s, e in segs:
        y_seg, state = ssd_scan(xs[:, s:e], dt[:, s:e], A, Bm[:, s:e], Cm[:, s:e], state, math.gcd(e - s, CHUNK))
        ys.append(y_seg)
    y = jnp.concatenate(ys, axis=1) + d_skip.astype(jnp.float32).reshape(N_GROUPS, HEADS_PER_GROUP)[:, :, None] * xs.astype(jnp.float32)
    y = y.reshape(bt, L, D_INNER)
    v = y * jax.nn.silu(z.astype(jnp.float32))
    v = v.reshape(bt, L, N_GROUPS, D_INNER // N_GROUPS)
    v = v * lax.rsqrt(jnp.mean(v * v, axis=-1, keepdims=True) + EPS)
    v = v.reshape(bt, L, D_INNER) * norm_g.astype(jnp.float32)
    out = v.astype(h.dtype) @ w_out
    return out, new_conv, state.reshape(bt, N_HEADS, HEAD_DIM, D_STATE)


def trunk(h, conv_cache, ssd_conv, ssm, n_lead, norm_mix, norm_ffn, norm_final,
          cf_w_pw1, cf_b_pw1, cf_w_dw, cf_b_dw, cf_ln_g, cf_ln_b, cf_w_pw2, cf_b_pw2,
          ssd_w_in, ssd_conv_w, ssd_conv_b, ssd_dt_bias, ssd_a_log, ssd_d, ssd_norm_g, ssd_w_out,
          ffn_w_gate, ffn_w_up, ffn_w_down):
    new_cc, new_sc, new_ss = [], [], []
    for i in range(DEPTH):
        j = i // N_MIXERS
        hn = rmsnorm(h, norm_mix[i])
        if i % N_MIXERS == 0:
            mix, cc = conformer_conv(hn, conv_cache[j], cf_w_pw1[j], cf_b_pw1[j], cf_w_dw[j], cf_b_dw[j],
                                     cf_ln_g[j], cf_ln_b[j], cf_w_pw2[j], cf_b_pw2[j])
            new_cc.append(cc)
        else:
            mix, sc, ss = ssd_mixer(hn, ssd_conv[j], ssm[j], n_lead, ssd_w_in[j], ssd_conv_w[j], ssd_conv_b[j],
                                    ssd_dt_bias[j], ssd_a_log[j], ssd_d[j], ssd_norm_g[j], ssd_w_out[j])
            new_sc.append(sc)
            new_ss.append(ss)
        h = h + mix
        h = h + swiglu(rmsnorm(h, norm_ffn[i]), ffn_w_gate[i], ffn_w_up[i], ffn_w_down[i])
    return rmsnorm(h, norm_final), jnp.stack(new_cc), jnp.stack(new_sc), jnp.stack(new_ss)


def setup_inputs(seed: int = 0) -> dict:
    key = jax.random.key(seed)
    ks = jax.random.split(key, 32)

    def nrm(k, shape, scale=1.0):
        return jax.random.normal(k, shape, jnp.float32) * scale

    NA, NB = N_CONV_LAYERS, N_SSD_LAYERS
    dt0 = jnp.exp(jax.random.uniform(ks[20], (NB, N_HEADS), jnp.float32, math.log(1e-3), math.log(1e-1)))
    return {
        'x_prompt': nrm(ks[0], (BATCH, SEQ, D_MODEL)),
        'x_sample': nrm(ks[1], (DEC_BATCH, DEC_SEQ, D_MODEL)),
        'cache_conv': nrm(ks[2], (NA, DEC_BATCH, CONV_HIST, D_CONV), 0.5),
        'state_ssd_conv': nrm(ks[3], (NB, DEC_BATCH, SSD_CONV_HIST, CONV_DIM)),
        'state_ssm': nrm(ks[4], (NB, DEC_BATCH, N_HEADS, HEAD_DIM, D_STATE), 0.1),
        'meta_tokens': nrm(ks[5], (N_META, D_MODEL)),
        'norm_mix': 1.0 + nrm(ks[6], (DEPTH, D_MODEL), 0.02),
        'norm_ffn': 1.0 + nrm(ks[7], (DEPTH, D_MODEL), 0.02),
        'norm_final': 1.0 + nrm(ks[8], (D_MODEL,), 0.02),
        'cf_w_pw1': nrm(ks[9], (NA, D_MODEL, 2 * D_CONV), D_MODEL ** -0.5),
        'cf_b_pw1': nrm(ks[10], (NA, 2 * D_CONV), 0.02),
        'cf_w_dw': nrm(ks[11], (NA, CONV_WIDTH, D_CONV), CONV_WIDTH ** -0.5),
        'cf_b_dw': nrm(ks[12], (NA, D_CONV), 0.02),
        'cf_ln_g': 1.0 + nrm(ks[13], (NA, D_CONV), 0.02),
        'cf_ln_b': nrm(ks[14], (NA, D_CONV), 0.02),
        'cf_w_pw2': nrm(ks[15], (NA, D_CONV, D_MODEL), D_CONV ** -0.5),
        'cf_b_pw2': nrm(ks[16], (NA, D_MODEL), 0.02),
        'ssd_w_in': nrm(ks[17], (NB, D_MODEL, IN_PROJ_DIM), D_MODEL ** -0.5),
        'ssd_conv_w': nrm(ks[18], (NB, SSD_CONV_WIDTH, CONV_DIM), SSD_CONV_WIDTH ** -0.5),
        'ssd_conv_b': nrm(ks[19], (NB, CONV_DIM), 0.02),
        'ssd_dt_bias': dt0 + jnp.log(-jnp.expm1(-dt0)),
        'ssd_a_log': jnp.log(jax.random.uniform(ks[21], (NB, N_HEADS), jnp.float32, 1.0, 16.0)),
        'ssd_d': 1.0 + nrm(ks[22], (NB, N_HEADS), 0.02),
        'ssd_norm_g': 1.0 + nrm(ks[23], (NB, D_INNER), 0.02),
        'ssd_w_out': nrm(ks[24], (NB, D_INNER, D_MODEL), D_INNER ** -0.5),
        'ffn_w_gate': nrm(ks[25], (DEPTH, D_MODEL, D_FF), D_MODEL ** -0.5),
        'ffn_w_up': nrm(ks[26], (DEPTH, D_MODEL, D_FF), D_MODEL ** -0.5),
        'ffn_w_down': nrm(ks[27], (DEPTH, D_FF, D_MODEL), D_FF ** -0.5),
    }


def reference(x_prompt, x_sample, cache_conv, state_ssd_conv, state_ssm, meta_tokens,
              norm_mix, norm_ffn, norm_final,
              cf_w_pw1, cf_b_pw1, cf_w_dw, cf_b_dw, cf_ln_g, cf_ln_b, cf_w_pw2, cf_b_pw2,
              ssd_w_in, ssd_conv_w, ssd_conv_b, ssd_dt_bias, ssd_a_log, ssd_d, ssd_norm_g, ssd_w_out,
              ffn_w_gate, ffn_w_up, ffn_w_down):
    weights = (norm_mix, norm_ffn, norm_final,
               cf_w_pw1, cf_b_pw1, cf_w_dw, cf_b_dw, cf_ln_g, cf_ln_b, cf_w_pw2, cf_b_pw2,
               ssd_w_in, ssd_conv_w, ssd_conv_b, ssd_dt_bias, ssd_a_log, ssd_d, ssd_norm_g, ssd_w_out,
               ffn_w_gate, ffn_w_up, ffn_w_down)
    bp = x_prompt.shape[0]
    meta = jnp.broadcast_to(meta_tokens.astype(x_prompt.dtype)[None], (bp, N_META, D_MODEL))
    hp = jnp.concatenate([meta, x_prompt], axis=1)
    cc0 = jnp.zeros((N_CONV_LAYERS, bp, CONV_HIST, D_CONV), x_prompt.dtype)
    sc0 = jnp.zeros((N_SSD_LAYERS, bp, SSD_CONV_HIST, CONV_DIM), x_prompt.dtype)
    ss0 = jnp.zeros((N_SSD_LAYERS, bp, N_HEADS, HEAD_DIM, D_STATE), jnp.float32)
    yp, cc_p, sc_p, ss_p = trunk(hp, cc0, sc0, ss0, N_META, *weights)
    y_prompt = yp[:, N_META:]
    y_sample, cc_s, sc_s, ss_s = trunk(x_sample, cache_conv, state_ssd_conv, state_ssm, 0, *weights)
    return (y_prompt, y_sample, cc_p, cc_s, sc_p, sc_s, ss_p, ss_s)
```

```python
import functools

import jax
import jax.numpy as jnp
from jax import lax
from jax.experimental import pallas as pl
from jax.experimental.pallas import tpu as pltpu

D_MODEL = 1024
N_META = 16
CONV_WIDTH = 31
CONV_HIST = CONV_WIDTH - 1
D_INNER = 2048
HEAD_DIM = 64
N_HEADS = 32
N_GROUPS = 8
HEADS_PER_GROUP = 4
GROUP_DIM = HEADS_PER_GROUP * HEAD_DIM
D_STATE = 128
SSD_CONV_WIDTH = 4
SSD_CONV_HIST = SSD_CONV_WIDTH - 1
CONV_DIM = D_INNER + 2 * N_GROUPS * D_STATE
D_FF = 2816
EPS = 1e-6

LANES = 128
SUBLANES = 8
HIST_PAD = 32
SSD_HIST_PAD = 8
VMEM_LIMIT = 56 * 1024 * 1024

F32 = jnp.float32
BF16 = jnp.bfloat16
_NT = (((1,), (1,)), ((), ()))


def _const_spec(shape):
    return pl.BlockSpec(shape, lambda *_: (0,) * len(shape), pipeline_mode=pl.Buffered(1))


def _params(n_axes):
    return pltpu.CompilerParams(dimension_semantics=("arbitrary",) * n_axes, vmem_limit_bytes=VMEM_LIMIT)


def _rms(x, g):
    return x * lax.rsqrt(jnp.mean(x * x, axis=-1, keepdims=True) + EPS) * g


def _sigmoid(x):
    return 1.0 / (1.0 + jnp.exp(-x))


def _silu(x):
    return x * _sigmoid(x)


def _dot(a, b):
    return jnp.dot(a, b, preferred_element_type=F32)


def _dwconv(src_ref, w_ref, bias_ref, dst_ref, *, tb, tl, rc, cc, taps, base, act):
    n_rc = tl // rc
    n_ch = src_ref.shape[-1]
    q_lo, q_hi = base // SUBLANES, (base + taps - 1) // SUBLANES

    def body(i, carry):
        b = i // n_rc
        r0 = pl.multiple_of((i % n_rc) * rc, SUBLANES)
        for c0 in range(0, n_ch, cc):
            acc = jnp.broadcast_to(bias_ref[:, c0:c0 + cc], (rc, cc))
            for q in range(q_lo, q_hi + 1):
                shifts = [s for s in range(SUBLANES) if 0 <= SUBLANES * q + s - base < taps]
                n_rows = rc + SUBLANES if shifts[-1] > 0 else rc
                rows = src_ref[b, pl.ds(r0 + SUBLANES * q, n_rows), c0:c0 + cc]
                for s in shifts:
                    k = SUBLANES * q + s - base
                    win = rows[:rc] if s == 0 else pltpu.roll(rows, n_rows - s, axis=0)[:rc]
                    acc = acc + win * pltpu.repeat(w_ref[k, :, c0:c0 + cc], rc // SUBLANES, axis=0)
            dst_ref[b, pl.ds(r0, rc), c0:c0 + cc] = act(acc)
        return carry

    lax.fori_loop(0, tb * n_rc, body, 0)


def _conformer_kernel(h_ref, hist_ref, gm_ref, w1_ref, b1_ref, wdw_ref, bdw_ref, lng_ref, lnb_ref, w2_ref, b2_ref,
                      out_ref, nh_ref, gh_ref, c_ref, *, tb, tl, rc):
    l = pl.program_id(1)
    m = tb * tl

    @pl.when(l == 0)
    def _():
        gh_ref[:, 0:HIST_PAD, :] = hist_ref[...]

    x = h_ref[...].reshape(m, D_MODEL)
    hn = _rms(x, gm_ref[...]).astype(BF16)
    u = _dot(hn, w1_ref[...]) + b1_ref[...]
    g = u[:, :D_MODEL] * _sigmoid(u[:, D_MODEL:])
    gh_ref[:, HIST_PAD:HIST_PAD + tl, :] = g.reshape(tb, tl, D_MODEL)

    _dwconv(gh_ref, wdw_ref, bdw_ref, c_ref, tb=tb, tl=tl, rc=rc, cc=D_MODEL,
            taps=CONV_WIDTH, base=HIST_PAD - CONV_HIST, act=lambda v: v)

    tail = gh_ref[:, tl:tl + HIST_PAD, :]
    gh_ref[:, 0:HIST_PAD, :] = tail

    @pl.when(l == pl.num_programs(1) - 1)
    def _():
        nh_ref[...] = tail

    c = c_ref[...].reshape(m, D_MODEL)
    mu = jnp.mean(c, axis=-1, keepdims=True)
    xc = c - mu
    cn = xc * lax.rsqrt(jnp.mean(xc * xc, axis=-1, keepdims=True) + EPS) * lng_ref[...] + lnb_ref[...]
    cn = _silu(cn).astype(BF16)
    y = _dot(cn, w2_ref[...]) + b2_ref[...] + x
    out_ref[...] = y.reshape(tb, tl, D_MODEL)


def _conformer(h, hist, gm, w1, b1, wdw, bdw, lng, lnb, w2, b2, *, tb, tl):
    bsz, seq, _ = h.shape
    rc = 16 if tl % 16 == 0 else SUBLANES
    kern = functools.partial(_conformer_kernel, tb=tb, tl=tl, rc=rc)
    return pl.pallas_call(
        kern,
        grid=(bsz // tb, seq // tl),
        in_specs=[
            pl.BlockSpec((tb, tl, D_MODEL), lambda b, l: (b, l, 0)),
            pl.BlockSpec((tb, HIST_PAD, D_MODEL), lambda b, l: (b, 0, 0)),
            _const_spec((1, D_MODEL)),
            _const_spec((D_MODEL, 2 * D_MODEL)),
            _const_spec((1, 2 * D_MODEL)),
            _const_spec((CONV_WIDTH, SUBLANES, D_MODEL)),
            _const_spec((1, D_MODEL)),
            _const_spec((1, D_MODEL)),
            _const_spec((1, D_MODEL)),
            _const_spec((D_MODEL, D_MODEL)),
            _const_spec((1, D_MODEL)),
        ],
        out_specs=[
            pl.BlockSpec((tb, tl, D_MODEL), lambda b, l: (b, l, 0)),
            pl.BlockSpec((tb, HIST_PAD, D_MODEL), lambda b, l: (b, 0, 0)),
        ],
        out_shape=[
            jax.ShapeDtypeStruct((bsz, seq, D_MODEL), F32),
            jax.ShapeDtypeStruct((bsz, HIST_PAD, D_MODEL), F32),
        ],
        scratch_shapes=[
            pltpu.VMEM((tb, HIST_PAD + tl, D_MODEL), F32),
            pltpu.VMEM((tb, tl, D_MODEL), F32),
        ],
        compiler_params=_params(2),
        name="conformer",
    )(h, hist, gm, w1, b1, wdw, bdw, lng, lnb, w2, b2)


def _ffn_kernel(x_ref, g_ref, wg_ref, wu_ref, wd_ref, gf_ref, o_ref, *, final):
    x = x_ref[...]
    hn = _rms(x, g_ref[...]).astype(BF16)
    a = _dot(hn, wg_ref[...])
    b = _dot(hn, wu_ref[...])
    t = (_silu(a) * b).astype(BF16)
    y = x + _dot(t, wd_ref[...])
    if final:
        y = _rms(y, gf_ref[...])
    o_ref[...] = y


def _ffn(x, g, wg, wu, wd, gf, *, tm, final):
    t = x.shape[0]
    return pl.pallas_call(
        functools.partial(_ffn_kernel, final=final),
        grid=(t // tm,),
        in_specs=[
            pl.BlockSpec((tm, D_MODEL), lambda i: (i, 0)),
            _const_spec((1, D_MODEL)),
            _const_spec((D_MODEL, D_FF)),
            _const_spec((D_MODEL, D_FF)),
            _const_spec((D_FF, D_MODEL)),
            _const_spec((1, D_MODEL)),
        ],
        out_specs=pl.BlockSpec((tm, D_MODEL), lambda i: (i, 0)),
        out_shape=jax.ShapeDtypeStruct((t, D_MODEL), F32),
        compiler_params=_params(1),
        name="ffn_final" if final else "ffn",
    )(x, g, wg, wu, wd, gf)


def _softplus(x):
    return jnp.maximum(x, 0.0) + jnp.log(1.0 + jnp.exp(-jnp.abs(x)))


def _ssd_front_kernel(h_ref, ch_ref, gm_ref, wz_ref, wx_ref, wdt_ref, cw_ref, cb_ref, dtb_ref,
                      z_ref, xbc_ref, dt_ref, nc_ref, xh_ref, *, tb, tl, rc):
    l = pl.program_id(1)
    m = tb * tl

    @pl.when(l == 0)
    def _():
        xh_ref[:, 0:SSD_HIST_PAD, :] = ch_ref[...]

    x = h_ref[...].reshape(m, D_MODEL)
    hn = _rms(x, gm_ref[...]).astype(BF16)
    z_ref[...] = _dot(hn, wz_ref[...]).reshape(tb, tl, D_INNER)
    xh_ref[:, SSD_HIST_PAD:SSD_HIST_PAD + tl, :] = _dot(hn, wx_ref[...]).reshape(tb, tl, CONV_DIM)
    dt_ref[...] = _softplus(_dot(hn, wdt_ref[...]) + dtb_ref[...]).reshape(tb, tl, LANES)

    _dwconv(xh_ref, cw_ref, cb_ref, xbc_ref, tb=tb, tl=tl, rc=rc, cc=1024,
            taps=SSD_CONV_WIDTH, base=SSD_HIST_PAD - SSD_CONV_HIST, act=_silu)

    tail = xh_ref[:, tl:tl + SSD_HIST_PAD, :]
    xh_ref[:, 0:SSD_HIST_PAD, :] = tail

    @pl.when(l == pl.num_programs(1) - 1)
    def _():
        nc_ref[...] = tail


def _ssd_front(h, ch, gm, wz, wx, wdt, cw, cb, dtb, *, tb, tl):
    bsz, seq, _ = h.shape
    rc = 16 if tl % 16 == 0 else SUBLANES
    return pl.pallas_call(
        functools.partial(_ssd_front_kernel, tb=tb, tl=tl, rc=rc),
        grid=(bsz // tb, seq // tl),
        in_specs=[
            pl.BlockSpec((tb, tl, D_MODEL), lambda b, l: (b, l, 0)),
            pl.BlockSpec((tb, SSD_HIST_PAD, CONV_DIM), lambda b, l: (b, 0, 0)),
            _const_spec((1, D_MODEL)),
            _const_spec((D_MODEL, D_INNER)),
            _const_spec((D_MODEL, CONV_DIM)),
            _const_spec((D_MODEL, LANES)),
            _const_spec((SSD_CONV_WIDTH, SUBLANES, CONV_DIM)),
            _const_spec((1, CONV_DIM)),
            _const_spec((1, LANES)),
        ],
        out_specs=[
            pl.BlockSpec((tb, tl, D_INNER), lambda b, l: (b, l, 0)),
            pl.BlockSpec((tb, tl, CONV_DIM), lambda b, l: (b, l, 0)),
            pl.BlockSpec((tb, tl, LANES), lambda b, l: (b, l, 0)),
            pl.BlockSpec((tb, SSD_HIST_PAD, CONV_DIM), lambda b, l: (b, 0, 0)),
        ],
        out_shape=[
            jax.ShapeDtypeStruct((bsz, seq, D_INNER), F32),
            jax.ShapeDtypeStruct((bsz, seq, CONV_DIM), F32),
            jax.ShapeDtypeStruct((bsz, seq, LANES), F32),
            jax.ShapeDtypeStruct((bsz, SSD_HIST_PAD, CONV_DIM), F32),
        ],
        scratch_shapes=[pltpu.VMEM((tb, SSD_HIST_PAD + tl, CONV_DIM), F32)],
        compiler_params=_params(2),
        name="ssd_front",
    )(h, ch, gm, wz, wx, wdt, cw, cb, dtb)


def _head_rows(col):
    out = []
    for g in range(N_GROUPS):
        parts = [jnp.broadcast_to(col[HEADS_PER_GROUP * g + r:HEADS_PER_GROUP * g + r + 1, :], (HEAD_DIM, D_STATE))
                 for r in range(HEADS_PER_GROUP)]
        out.append(jnp.concatenate(parts, axis=0))
    return out


def _ssd_scan_kernel(xbc_ref, dt_ref, h0_ref, alog_ref, dsk_ref, y_ref, st_ref, *, lc):
    c = pl.program_id(1)

    @pl.when(c == 0)
    def _():
        st_ref[...] = h0_ref[...]

    dt = dt_ref[0]
    a = dt * (-jnp.exp(alog_ref[...]))
    row = lax.broadcasted_iota(jnp.int32, (lc, lc), 0)
    col = lax.broadcasted_iota(jnp.int32, (lc, lc), 1)
    causal = row >= col
    acs = jnp.dot(causal.astype(F32), a, preferred_element_type=F32, precision=lax.Precision.HIGHEST)
    acs_t = acs.T
    coef = jnp.exp(acs[lc - 1:lc, :] - acs) * dt
    eacs = jnp.exp(acs)
    scales = _head_rows(jnp.exp(acs_t[:, lc - 1:lc]))

    for g in range(N_GROUPS):
        b_off = D_INNER + g * D_STATE
        c_off = D_INNER + N_GROUPS * D_STATE + g * D_STATE
        bg = xbc_ref[0, :, b_off:b_off + D_STATE].astype(BF16)
        cg = xbc_ref[0, :, c_off:c_off + D_STATE].astype(BF16)
        xg = xbc_ref[0, :, g * GROUP_DIM:(g + 1) * GROUP_DIM]
        cb = lax.dot_general(cg, bg, _NT, preferred_element_type=F32)
        hg = st_ref[0, g]
        yoff = lax.dot_general(cg, hg.astype(BF16), _NT, preferred_element_type=F32)
        ys, xcs = [], []
        for r in range(HEADS_PER_GROUP):
            h = HEADS_PER_GROUP * g + r
            decay = jnp.exp(jnp.where(causal, acs[:, h:h + 1] - acs_t[h:h + 1, :], -jnp.inf))
            w = (cb * decay).astype(BF16)
            xr = xg[:, r * HEAD_DIM:(r + 1) * HEAD_DIM]
            yd = _dot(w, (xr * dt[:, h:h + 1]).astype(BF16))
            ys.append(yd + eacs[:, h:h + 1] * yoff[:, r * HEAD_DIM:(r + 1) * HEAD_DIM] + dsk_ref[:, h:h + 1] * xr)
            xcs.append(xr * coef[:, h:h + 1])
        y_ref[0, :, g * GROUP_DIM:(g + 1) * GROUP_DIM] = jnp.concatenate(ys, axis=1)
        xcoef_t = jnp.concatenate(xcs, axis=1).T.astype(BF16)
        st_ref[0, g] = scales[g] * hg + _dot(xcoef_t, bg)


def _ssd_scan(xbc, dt, h0, alog, dsk, *, lc):
    bsz, seq, _ = xbc.shape
    return pl.pallas_call(
        functools.partial(_ssd_scan_kernel, lc=lc),
        grid=(bsz, seq // lc),
        in_specs=[
            pl.BlockSpec((1, lc, CONV_DIM), lambda b, c: (b, c, 0)),
            pl.BlockSpec((1, lc, LANES), lambda b, c: (b, c, 0)),
            pl.BlockSpec((1, N_GROUPS, GROUP_DIM, D_STATE), lambda b, c: (b, 0, 0, 0)),
            _const_spec((1, LANES)),
            _const_spec((1, LANES)),
        ],
        out_specs=[
            pl.BlockSpec((1, lc, D_INNER), lambda b, c: (b, c, 0)),
            pl.BlockSpec((1, N_GROUPS, GROUP_DIM, D_STATE), lambda b, c: (b, 0, 0, 0)),
        ],
        out_shape=[
            jax.ShapeDtypeStruct((bsz, seq, D_INNER), F32),
            jax.ShapeDtypeStruct((bsz, N_GROUPS, GROUP_DIM, D_STATE), F32),
        ],
        compiler_params=_params(2),
        name="ssd_scan",
    )(xbc, dt, h0, alog, dsk)


def _ssd_step_kernel(x_ref, b_ref, c_ref, dt_ref, st_ref, alog_ref, dsk_ref, y_ref, so_ref, *, nb, ls):
    g = pl.program_id(1)
    rows = nb * ls
    shift = (LANES - HEADS_PER_GROUP * g) % LANES
    dt = pltpu.roll(dt_ref[...], shift, axis=1)
    alog = pltpu.roll(alog_ref[...], shift, axis=1)
    dsk = pltpu.roll(dsk_ref[...], shift, axis=1)
    a = dt * (-jnp.exp(alog))

    row = lax.broadcasted_iota(jnp.int32, (rows, rows), 0)
    col = lax.broadcasted_iota(jnp.int32, (rows, rows), 1)
    same = (row // ls) == (col // ls)
    causal = jnp.logical_and(same, row >= col)
    hi = lax.Precision.HIGHEST
    acs = jnp.dot(jnp.where(causal, 1.0, 0.0), a, preferred_element_type=F32, precision=hi)
    tot = jnp.dot(jnp.where(same, 1.0, 0.0), a, preferred_element_type=F32, precision=hi)
    acs_t = acs.T
    coef = jnp.exp(tot - acs) * dt
    eacs = jnp.exp(acs)
    etot = jnp.exp(tot)

    bg = b_ref[...].astype(BF16)
    cg = c_ref[...].astype(BF16)
    xg = x_ref[...]
    cb = lax.dot_general(cg, bg, _NT, preferred_element_type=F32)

    seq_of_col = lax.broadcasted_iota(jnp.int32, (GROUP_DIM, rows), 1) // ls
    yoff_t = jnp.zeros((GROUP_DIM, rows), F32)
    for b in range(nb):
        yb = lax.dot_general(st_ref[b, 0].astype(BF16), cg, _NT, preferred_element_type=F32)
        yoff_t = jnp.where(seq_of_col == b, yb, yoff_t)
    yoff = yoff_t.T

    ys, xcs = [], []
    for r in range(HEADS_PER_GROUP):
        decay = jnp.exp(jnp.where(causal, acs[:, r:r + 1] - acs_t[r:r + 1, :], -jnp.inf))
        w = (cb * decay).astype(BF16)
        xr = xg[:, r * HEAD_DIM:(r + 1) * HEAD_DIM]
        yd = _dot(w, (xr * dt[:, r:r + 1]).astype(BF16))
        ys.append(yd + eacs[:, r:r + 1] * yoff[:, r * HEAD_DIM:(r + 1) * HEAD_DIM] + dsk[:, r:r + 1] * xr)
        xcs.append(xr * coef[:, r:r + 1])
    y_ref[...] = jnp.concatenate(ys, axis=1)
    xcoef_t = jnp.concatenate(xcs, axis=1).T

    for b in range(nb):
        upd = _dot(jnp.where(seq_of_col == b, xcoef_t, 0.0).astype(BF16), bg)
        scale = jnp.concatenate(
            [jnp.broadcast_to(etot[b * ls:b * ls + 1, r:r + 1], (HEAD_DIM, D_STATE)) for r in range(HEADS_PER_GROUP)],
            axis=0)
        so_ref[b, 0] = scale * st_ref[b, 0] + upd


def _ssd_step(xbc, dt, st, alog, dsk, *, nb, ls):
    t = xbc.shape[0]
    n_seq = st.shape[0]
    rows = nb * ls
    b_blk0 = D_INNER // D_STATE
    c_blk0 = b_blk0 + N_GROUPS
    return pl.pallas_call(
        functools.partial(_ssd_step_kernel, nb=nb, ls=ls),
        grid=(n_seq // nb, N_GROUPS),
        in_specs=[
            pl.BlockSpec((rows, GROUP_DIM), lambda i, g: (i, g)),
            pl.BlockSpec((rows, D_STATE), lambda i, g: (i, b_blk0 + g)),
            pl.BlockSpec((rows, D_STATE), lambda i, g: (i, c_blk0 + g)),
            pl.BlockSpec((rows, LANES), lambda i, g: (i, 0)),
            pl.BlockSpec((nb, 1, GROUP_DIM, D_STATE), lambda i, g: (i, g, 0, 0)),
            _const_spec((1, LANES)),
            _const_spec((1, LANES)),
        ],
        out_specs=[
            pl.BlockSpec((rows, GROUP_DIM), lambda i, g: (i, g)),
            pl.BlockSpec((nb, 1, GROUP_DIM, D_STATE), lambda i, g: (i, g, 0, 0)),
        ],
        out_shape=[
            jax.ShapeDtypeStruct((t, D_INNER), F32),
            jax.ShapeDtypeStruct((n_seq, N_GROUPS, GROUP_DIM, D_STATE), F32),
        ],
        compiler_params=_params(2),
        name="ssd_step",
    )(xbc, xbc, xbc, dt, st, alog, dsk)


def _ssd_post_kernel(y_ref, z_ref, h_ref, ng_ref, wo_ref, o_ref):
    acc = h_ref[...]
    for g in range(N_GROUPS):
        lo, hi = g * GROUP_DIM, (g + 1) * GROUP_DIM
        v = y_ref[:, lo:hi] * _silu(z_ref[:, lo:hi])
        vn = v * lax.rsqrt(jnp.mean(v * v, axis=-1, keepdims=True) + EPS) * ng_ref[:, lo:hi]
        acc = acc + _dot(vn.astype(BF16), wo_ref[lo:hi, :])
    o_ref[...] = acc


def _ssd_post(y, z, h, ng, wo, *, tm):
    t = y.shape[0]
    return pl.pallas_call(
        _ssd_post_kernel,
        grid=(t // tm,),
        in_specs=[
            pl.BlockSpec((tm, D_INNER), lambda i: (i, 0)),
            pl.BlockSpec((tm, D_INNER), lambda i: (i, 0)),
            pl.BlockSpec((tm, D_MODEL), lambda i: (i, 0)),
            _const_spec((1, D_INNER)),
            _const_spec((D_INNER, D_MODEL)),
        ],
        out_specs=pl.BlockSpec((tm, D_MODEL), lambda i: (i, 0)),
        out_shape=jax.ShapeDtypeStruct((t, D_MODEL), F32),
        compiler_params=_params(1),
        name="ssd_post",
    )(y, z, h, ng, wo)


def _trunk(h, cc_hist, sc_hist, ssm, w, *, tb, tl, tm, short):
    bsz, seq, _ = h.shape
    t = bsz * seq
    h1, new_cc = _conformer(h, cc_hist, w["gm0"], w["w_pw1"], w["b_pw1"], w["w_dw"], w["b_dw"], w["ln_g"], w["ln_b"],
                            w["w_pw2"], w["b_pw2"], tb=tb, tl=tl)
    h2 = _ffn(h1.reshape(t, D_MODEL), w["gf0"], w["wg0"], w["wu0"], w["wd0"], w["g_final"], tm=tm, final=False)
    z, xbc, dt, new_sc = _ssd_front(h2.reshape(bsz, seq, D_MODEL), sc_hist, w["gm1"], w["w_z"], w["w_xbc"], w["w_dt"],
                                    w["conv_w"], w["conv_b"], w["dt_bias"], tb=tb, tl=tl)
    if short:
        y, new_ss = _ssd_step(xbc.reshape(t, CONV_DIM), dt.reshape(t, LANES), ssm, w["a_log"], w["d_skip"],
                              nb=LANES // seq, ls=seq)
    else:
        pad = (-seq) % LANES
        if pad:
            xbc = jnp.pad(xbc, ((0, 0), (0, pad), (0, 0)))
            dt = jnp.pad(dt, ((0, 0), (0, pad), (0, 0)))
        y, new_ss = _ssd_scan(xbc, dt, ssm, w["a_log"], w["d_skip"], lc=LANES)
        y = y[:, :seq].reshape(t, D_INNER)
    h3 = _ssd_post(y, z.reshape(t, D_INNER), h2, w["norm_g"], w["w_out"], tm=tm)
    out = _ffn(h3, w["gf1"], w["wg1"], w["wu1"], w["wd1"], w["g_final"], tm=tm, final=True)
    return out.reshape(bsz, seq, D_MODEL), new_cc, new_sc, new_ss


def _pad_lanes(v):
    return jnp.pad(v.astype(F32), (0, LANES - v.shape[0])).reshape(1, LANES)


def kernel(x_prompt, x_sample, cache_conv, state_ssd_conv, state_ssm, meta_tokens, norm_mix, norm_ffn, norm_final, cf_w_pw1, cf_b_pw1, cf_w_dw, cf_b_dw, cf_ln_g, cf_ln_b, cf_w_pw2, cf_b_pw2, ssd_w_in, ssd_conv_w, ssd_conv_b, ssd_dt_bias, ssd_a_log, ssd_d, ssd_norm_g, ssd_w_out, ffn_w_gate, ffn_w_up, ffn_w_down):
    row = lambda v: v.astype(F32).reshape(1, -1)
    w_in = ssd_w_in[0]
    w = {
        "gm0": row(norm_mix[0]), "gm1": row(norm_mix[1]),
        "gf0": row(norm_ffn[0]), "gf1": row(norm_ffn[1]), "g_final": row(norm_final),
        "w_pw1": cf_w_pw1[0].astype(BF16), "b_pw1": row(cf_b_pw1[0]),
        "w_dw": jnp.broadcast_to(cf_w_dw[0][:, None, :], (CONV_WIDTH, SUBLANES, D_MODEL)),
        "b_dw": row(cf_b_dw[0]), "ln_g": row(cf_ln_g[0]), "ln_b": row(cf_ln_b[0]),
        "w_pw2": cf_w_pw2[0].astype(BF16), "b_pw2": row(cf_b_pw2[0]),
        "w_z": w_in[:, :D_INNER].astype(BF16),
        "w_xbc": w_in[:, D_INNER:D_INNER + CONV_DIM].astype(BF16),
        "w_dt": jnp.pad(w_in[:, D_INNER + CONV_DIM:], ((0, 0), (0, LANES - N_HEADS))).astype(BF16),
        "conv_w": jnp.broadcast_to(ssd_conv_w[0][:, None, :], (SSD_CONV_WIDTH, SUBLANES, CONV_DIM)),
        "conv_b": row(ssd_conv_b[0]),
        "dt_bias": _pad_lanes(ssd_dt_bias[0]), "a_log": _pad_lanes(ssd_a_log[0]), "d_skip": _pad_lanes(ssd_d[0]),
        "norm_g": row(ssd_norm_g[0]), "w_out": ssd_w_out[0].astype(BF16),
        "wg0": ffn_w_gate[0].astype(BF16), "wu0": ffn_w_up[0].astype(BF16), "wd0": ffn_w_down[0].astype(BF16),
        "wg1": ffn_w_gate[1].astype(BF16), "wu1": ffn_w_up[1].astype(BF16), "wd1": ffn_w_down[1].astype(BF16),
    }
    bp, seq, _ = x_prompt.shape
    bs, dseq, _ = x_sample.shape

    _, cc_m, sc_m, ss_m = _trunk(
        meta_tokens.astype(F32)[None], jnp.zeros((1, HIST_PAD, D_MODEL), F32),
        jnp.zeros((1, SSD_HIST_PAD, CONV_DIM), F32), jnp.zeros((1, N_GROUPS, GROUP_DIM, D_STATE), F32), w,
        tb=1, tl=N_META, tm=N_META, short=False)

    y_prompt, cc_p, sc_p, ss_p = _trunk(
        x_prompt, jnp.broadcast_to(cc_m, (bp, HIST_PAD, D_MODEL)),
        jnp.broadcast_to(sc_m, (bp, SSD_HIST_PAD, CONV_DIM)),
        jnp.broadcast_to(ss_m, (bp, N_GROUPS, GROUP_DIM, D_STATE)), w,
        tb=1, tl=256, tm=512, short=False)

    cc_in = jnp.pad(cache_conv[0], ((0, 0), (HIST_PAD - CONV_HIST, 0), (0, 0)))
    sc_in = jnp.pad(state_ssd_conv[0], ((0, 0), (SSD_HIST_PAD - SSD_CONV_HIST, 0), (0, 0)))
    y_sample, cc_s, sc_s, ss_s = _trunk(
        x_sample, cc_in, sc_in, state_ssm[0].reshape(bs, N_GROUPS, GROUP_DIM, D_STATE), w,
        tb=32, tl=dseq, tm=512, short=True)

    unpad_cc = lambda v: v[None, :, HIST_PAD - CONV_HIST:, :]
    unpad_sc = lambda v: v[None, :, SSD_HIST_PAD - SSD_CONV_HIST:, :]
    unpack_ss = lambda v: v.reshape(1, v.shape[0], N_HEADS, HEAD_DIM, D_STATE)
    return (y_prompt, y_sample, unpad_cc(cc_p), unpad_cc(cc_s), unpad_sc(sc_p), unpad_sc(sc_s),
            unpack_ss(ss_p), unpack_ss(ss_s))
```

```python
import functools

import jax
import jax.numpy as jnp
from jax import lax
from jax.experimental import pallas as pl
from jax.experimental.pallas import tpu as pltpu

D_MODEL = 1024
N_META = 16
CONV_WIDTH = 31
CONV_HIST = CONV_WIDTH - 1
D_INNER = 2048
HEAD_DIM = 64
N_HEADS = 32
N_GROUPS = 8
HEADS_PER_GROUP = 4
GROUP_DIM = HEADS_PER_GROUP * HEAD_DIM
D_STATE = 128
SSD_CONV_WIDTH = 4
SSD_CONV_HIST = SSD_CONV_WIDTH - 1
CONV_DIM = D_INNER + 2 * N_GROUPS * D_STATE
D_FF = 2816
EPS = 1e-6

LANES = 128
SUBLANES = 8
HIST_PAD = 32
SSD_HIST_PAD = 8
VMEM_LIMIT = 56 * 1024 * 1024

F32 = jnp.float32
BF16 = jnp.bfloat16
_NT = (((1,), (1,)), ((), ()))


def _const_spec(shape):
    return pl.BlockSpec(shape, lambda *_: (0,) * len(shape), pipeline_mode=pl.Buffered(1))


def _params(n_axes):
    return pltpu.CompilerParams(dimension_semantics=("arbitrary",) * n_axes, vmem_limit_bytes=VMEM_LIMIT)


def _rms(x, g):
    return x * lax.rsqrt(jnp.mean(x * x, axis=-1, keepdims=True) + EPS) * g


def _sigmoid(x):
    return 1.0 / (1.0 + jnp.exp(-x))


def _silu(x):
    return x * _sigmoid(x)


def _dot(a, b):
    return jnp.dot(a, b, preferred_element_type=F32)


def _dwconv(src_ref, w_ref, bias_ref, dst_ref, *, tb, tl, rc, cc, taps, base, act):
    n_rc = tl // rc
    n_ch = src_ref.shape[-1]
    q_lo, q_hi = base // SUBLANES, (base + taps - 1) // SUBLANES

    def body(i, carry):
        b = i // n_rc
        r0 = pl.multiple_of((i % n_rc) * rc, SUBLANES)
        for c0 in range(0, n_ch, cc):
            acc = jnp.broadcast_to(bias_ref[:, c0:c0 + cc], (rc, cc))
            for q in range(q_lo, q_hi + 1):
                shifts = [s for s in range(SUBLANES) if 0 <= SUBLANES * q + s - base < taps]
                n_rows = rc + SUBLANES if shifts[-1] > 0 else rc
                rows = src_ref[b, pl.ds(r0 + SUBLANES * q, n_rows), c0:c0 + cc]
                for s in shifts:
                    k = SUBLANES * q + s - base
                    win = rows[:rc] if s == 0 else pltpu.roll(rows, n_rows - s, axis=0)[:rc]
                    acc = acc + win * pltpu.repeat(w_ref[k, :, c0:c0 + cc], rc // SUBLANES, axis=0)
            dst_ref[b, pl.ds(r0, rc), c0:c0 + cc] = act(acc)
        return carry

    lax.fori_loop(0, tb * n_rc, body, 0)


def _conformer_kernel(h_ref, hist_ref, gm_ref, w1_ref, b1_ref, wdw_ref, bdw_ref, lng_ref, lnb_ref, w2_ref, b2_ref,
                      out_ref, nh_ref, gh_ref, c_ref, *, tb, tl, rc):
    l = pl.program_id(1)
    m = tb * tl

    @pl.when(l == 0)
    def _():
        gh_ref[:, 0:HIST_PAD, :] = hist_ref[...]

    x = h_ref[...].reshape(m, D_MODEL)
    hn = _rms(x, gm_ref[...]).astype(BF16)
    u = _dot(hn, w1_ref[...]) + b1_ref[...]
    g = u[:, :D_MODEL] * _sigmoid(u[:, D_MODEL:])
    gh_ref[:, HIST_PAD:HIST_PAD + tl, :] = g.reshape(tb, tl, D_MODEL)

    _dwconv(gh_ref, wdw_ref, bdw_ref, c_ref, tb=tb, tl=tl, rc=rc, cc=D_MODEL,
            taps=CONV_WIDTH, base=HIST_PAD - CONV_HIST, act=lambda v: v)

    tail = gh_ref[:, tl:tl + HIST_PAD, :]
    gh_ref[:, 0:HIST_PAD, :] = tail

    @pl.when(l == pl.num_programs(1) - 1)
    def _():
        nh_ref[...] = tail

    c = c_ref[...].reshape(m, D_MODEL)
    mu = jnp.mean(c, axis=-1, keepdims=True)
    xc = c - mu
    cn = xc * lax.rsqrt(jnp.mean(xc * xc, axis=-1, keepdims=True) + EPS) * lng_ref[...] + lnb_ref[...]
    cn = _silu(cn).astype(BF16)
    y = _dot(cn, w2_ref[...]) + b2_ref[...] + x
    out_ref[...] = y.reshape(tb, tl, D_MODEL)


def _conformer(h, hist, gm, w1, b1, wdw, bdw, lng, lnb, w2, b2, *, tb, tl):
    bsz, seq, _ = h.shape
    rc = 16 if tl % 16 == 0 else SUBLANES
    kern = functools.partial(_conformer_kernel, tb=tb, tl=tl, rc=rc)
    return pl.pallas_call(
        kern,
        grid=(bsz // tb, seq // tl),
        in_specs=[
            pl.BlockSpec((tb, tl, D_MODEL), lambda b, l: (b, l, 0)),
            pl.BlockSpec((tb, HIST_PAD, D_MODEL), lambda b, l: (b, 0, 0)),
            _const_spec((1, D_MODEL)),
            _const_spec((D_MODEL, 2 * D_MODEL)),
            _const_spec((1, 2 * D_MODEL)),
            _const_spec((CONV_WIDTH, SUBLANES, D_MODEL)),
            _const_spec((1, D_MODEL)),
            _const_spec((1, D_MODEL)),
            _const_spec((1, D_MODEL)),
            _const_spec((D_MODEL, D_MODEL)),
            _const_spec((1, D_MODEL)),
        ],
        out_specs=[
            pl.BlockSpec((tb, tl, D_MODEL), lambda b, l: (b, l, 0)),
            pl.BlockSpec((tb, HIST_PAD, D_MODEL), lambda b, l: (b, 0, 0)),
        ],
        out_shape=[
            jax.ShapeDtypeStruct((bsz, seq, D_MODEL), F32),
            jax.ShapeDtypeStruct((bsz, HIST_PAD, D_MODEL), F32),
        ],
        scratch_shapes=[
            pltpu.VMEM((tb, HIST_PAD + tl, D_MODEL), F32),
            pltpu.VMEM((tb, tl, D_MODEL), F32),
        ],
        compiler_params=_params(2),
        name="conformer",
    )(h, hist, gm, w1, b1, wdw, bdw, lng, lnb, w2, b2)


def _ffn_kernel(x_ref, g_ref, wg_ref, wu_ref, wd_ref, gf_ref, o_ref, *, final):
    x = x_ref[...]
    hn = _rms(x, g_ref[...]).astype(BF16)
    a = _dot(hn, wg_ref[...])
    b = _dot(hn, wu_ref[...])
    t = (_silu(a) * b).astype(BF16)
    y = x + _dot(t, wd_ref[...])
    if final:
        y = _rms(y, gf_ref[...])
    o_ref[...] = y


def _ffn(x, g, wg, wu, wd, gf, *, tm, final):
    t = x.shape[0]
    return pl.pallas_call(
        functools.partial(_ffn_kernel, final=final),
        grid=(t // tm,),
        in_specs=[
            pl.BlockSpec((tm, D_MODEL), lambda i: (i, 0)),
            _const_spec((1, D_MODEL)),
            _const_spec((D_MODEL, D_FF)),
            _const_spec((D_MODEL, D_FF)),
            _const_spec((D_FF, D_MODEL)),
            _const_spec((1, D_MODEL)),
        ],
        out_specs=pl.BlockSpec((tm, D_MODEL), lambda i: (i, 0)),
        out_shape=jax.ShapeDtypeStruct((t, D_MODEL), F32),
        compiler_params=_params(1),
        name="ffn_final" if final else "ffn",
    )(x, g, wg, wu, wd, gf)


def _softplus(x):
    return jnp.maximum(x, 0.0) + jnp.log(1.0 + jnp.exp(-jnp.abs(x)))


def _ssd_front_kernel(h_ref, ch_ref, gm_ref, wz_ref, wx_ref, wdt_ref, cw_ref, cb_ref, dtb_ref,
                      z_ref, xbc_ref, dt_ref, nc_ref, xh_ref, *, tb, tl, rc):
    l = pl.program_id(1)
    m = tb * tl

    @pl.when(l == 0)
    def _():
        xh_ref[:, 0:SSD_HIST_PAD, :] = ch_ref[...]

    x = h_ref[...].reshape(m, D_MODEL)
    hn = _rms(x, gm_ref[...]).astype(BF16)
    z_ref[...] = _dot(hn, wz_ref[...]).reshape(tb, tl, D_INNER)
    xh_ref[:, SSD_HIST_PAD:SSD_HIST_PAD + tl, :] = _dot(hn, wx_ref[...]).reshape(tb, tl, CONV_DIM)
    dt_ref[...] = _softplus(_dot(hn, wdt_ref[...]) + dtb_ref[...]).reshape(tb, tl, LANES)

    _dwconv(xh_ref, cw_ref, cb_ref, xbc_ref, tb=tb, tl=tl, rc=rc, cc=1024,
            taps=SSD_CONV_WIDTH, base=SSD_HIST_PAD - SSD_CONV_HIST, act=_silu)

    tail = xh_ref[:, tl:tl + SSD_HIST_PAD, :]
    xh_ref[:, 0:SSD_HIST_PAD, :] = tail

    @pl.when(l == pl.num_programs(1) - 1)
    def _():
        nc_ref[...] = tail


def _ssd_front(h, ch, gm, wz, wx, wdt, cw, cb, dtb, *, tb, tl):
    bsz, seq, _ = h.shape
    rc = 16 if tl % 16 == 0 else SUBLANES
    return pl.pallas_call(
        functools.partial(_ssd_front_kernel, tb=tb, tl=tl, rc=rc),
        grid=(bsz // tb, seq // tl),
        in_specs=[
            pl.BlockSpec((tb, tl, D_MODEL), lambda b, l: (b, l, 0)),
            pl.BlockSpec((tb, SSD_HIST_PAD, CONV_DIM), lambda b, l: (b, 0, 0)),
            _const_spec((1, D_MODEL)),
            _const_spec((D_MODEL, D_INNER)),
            _const_spec((D_MODEL, CONV_DIM)),
            _const_spec((D_MODEL, LANES)),
            _const_spec((SSD_CONV_WIDTH, SUBLANES, CONV_DIM)),
            _const_spec((1, CONV_DIM)),
            _const_spec((1, LANES)),
        ],
        out_specs=[
            pl.BlockSpec((tb, tl, D_INNER), lambda b, l: (b, l, 0)),
            pl.BlockSpec((tb, tl, CONV_DIM), lambda b, l: (b, l, 0)),
            pl.BlockSpec((tb, tl, LANES), lambda b, l: (b, l, 0)),
            pl.BlockSpec((tb, SSD_HIST_PAD, CONV_DIM), lambda b, l: (b, 0, 0)),
        ],
        out_shape=[
            jax.ShapeDtypeStruct((bsz, seq, D_INNER), F32),
            jax.ShapeDtypeStruct((bsz, seq, CONV_DIM), F32),
            jax.ShapeDtypeStruct((bsz, seq, LANES), F32),
            jax.ShapeDtypeStruct((bsz, SSD_HIST_PAD, CONV_DIM), F32),
        ],
        scratch_shapes=[pltpu.VMEM((tb, SSD_HIST_PAD + tl, CONV_DIM), F32)],
        compiler_params=_params(2),
        name="ssd_front",
    )(h, ch, gm, wz, wx, wdt, cw, cb, dtb)


def _cumsum_rows(a, tri):
    a_hi = a.astype(BF16)
    r1 = a - a_hi.astype(F32)
    a_mid = r1.astype(BF16)
    a_lo = (r1 - a_mid.astype(F32)).astype(BF16)
    return _dot(tri, a_hi) + _dot(tri, a_mid) + _dot(tri, a_lo)


def _head_rows(mat, g, width):
    return jnp.concatenate(
        [jnp.broadcast_to(mat[HEADS_PER_GROUP * g + r:HEADS_PER_GROUP * g + r + 1, :], (HEAD_DIM, width))
         for r in range(HEADS_PER_GROUP)], axis=0)


def _ssd_scan_kernel(xbc_ref, dt_ref, z_ref, h_ref, h0_ref, alog_ref, dske_ref, ng_ref, wo_ref, o_ref, st_ref, *, lc):
    c = pl.program_id(1)

    @pl.when(c == 0)
    def _():
        st_ref[...] = h0_ref[...]

    dt = dt_ref[0]
    a = dt * (-jnp.exp(alog_ref[...]))
    row = lax.broadcasted_iota(jnp.int32, (lc, lc), 0)
    col = lax.broadcasted_iota(jnp.int32, (lc, lc), 1)
    causal = row >= col
    acs = _cumsum_rows(a, jnp.where(causal, 1.0, 0.0).astype(BF16))
    acs_t = acs.T
    dt_t = dt.T
    last_t = jnp.broadcast_to(acs_t[:, lc - 1:lc], (LANES, lc))
    c_t = acs_t - jnp.log(dt_t)
    coef_t = jnp.exp(last_t - acs_t) * dt_t
    cd = jnp.broadcast_to(jnp.exp(acs_t[:, lc - 1:lc]), (LANES, D_STATE))
    lane_head = lax.broadcasted_iota(jnp.int32, (lc, GROUP_DIM), 1) // HEAD_DIM
    lane_lo = lax.broadcasted_iota(jnp.int32, (lc, LANES), 1) < HEAD_DIM

    acc = h_ref[0]
    for g in range(N_GROUPS):
        lo, hi = g * GROUP_DIM, (g + 1) * GROUP_DIM
        b_off = D_INNER + g * D_STATE
        c_off = D_INNER + N_GROUPS * D_STATE + g * D_STATE
        bg = xbc_ref[0, :, b_off:b_off + D_STATE].astype(BF16)
        cg = xbc_ref[0, :, c_off:c_off + D_STATE].astype(BF16)
        xg = xbc_ref[0, :, lo:hi]
        cb = lax.dot_general(cg, bg, _NT, preferred_element_type=F32)
        hg = st_ref[0, g]
        yoff = lax.dot_general(cg, hg.astype(BF16), _NT, preferred_element_type=F32)

        ws, acs_b = [], []
        for r in range(HEADS_PER_GROUP):
            h = HEADS_PER_GROUP * g + r
            ab = jnp.broadcast_to(acs[:, h:h + 1], (lc, LANES))
            acs_b.append(ab)
            ws.append((cb * jnp.exp(jnp.where(causal, ab - c_t[h:h + 1, :], -jnp.inf))).astype(BF16))
        x_bd = jnp.concatenate([jnp.where(lane_head == r, xg, 0.0).astype(BF16) for r in range(HEADS_PER_GROUP)],
                               axis=0)
        yd = _dot(jnp.concatenate(ws, axis=1), x_bd)
        e_b = jnp.exp(jnp.concatenate([jnp.where(lane_lo, acs_b[0], acs_b[1]),
                                       jnp.where(lane_lo, acs_b[2], acs_b[3])], axis=1))
        y = yd + e_b * yoff + dske_ref[:, lo:hi] * xg

        xcoef_t = (xg.T * _head_rows(coef_t, g, lc)).astype(BF16)
        st_ref[0, g] = _head_rows(cd, g, D_STATE) * hg + _dot(xcoef_t, bg)

        v = y * _silu(z_ref[0, :, lo:hi])
        vn = v * lax.rsqrt(jnp.mean(v * v, axis=-1, keepdims=True) + EPS) * ng_ref[:, lo:hi]
        acc = acc + _dot(vn.astype(BF16), wo_ref[lo:hi, :])
    o_ref[0] = acc


def _ssd_scan(xbc, dt, z, h, h0, alog, dske, ng, wo, *, lc):
    bsz, seq, _ = xbc.shape
    return pl.pallas_call(
        functools.partial(_ssd_scan_kernel, lc=lc),
        grid=(bsz, seq // lc),
        in_specs=[
            pl.BlockSpec((1, lc, CONV_DIM), lambda b, c: (b, c, 0)),
            pl.BlockSpec((1, lc, LANES), lambda b, c: (b, c, 0)),
            pl.BlockSpec((1, lc, D_INNER), lambda b, c: (b, c, 0)),
            pl.BlockSpec((1, lc, D_MODEL), lambda b, c: (b, c, 0)),
            pl.BlockSpec((1, N_GROUPS, GROUP_DIM, D_STATE), lambda b, c: (b, 0, 0, 0)),
            _const_spec((1, LANES)),
            _const_spec((1, D_INNER)),
            _const_spec((1, D_INNER)),
            _const_spec((D_INNER, D_MODEL)),
        ],
        out_specs=[
            pl.BlockSpec((1, lc, D_MODEL), lambda b, c: (b, c, 0)),
            pl.BlockSpec((1, N_GROUPS, GROUP_DIM, D_STATE), lambda b, c: (b, 0, 0, 0)),
        ],
        out_shape=[
            jax.ShapeDtypeStruct((bsz, seq, D_MODEL), F32),
            jax.ShapeDtypeStruct((bsz, N_GROUPS, GROUP_DIM, D_STATE), F32),
        ],
        compiler_params=_params(2),
        name="ssd_scan",
    )(xbc, dt, z, h, h0, alog, dske, ng, wo)


def _ssd_step_kernel(x_ref, b_ref, c_ref, dt_ref, st_ref, alog_ref, dsk_ref, y_ref, so_ref, *, nb, ls):
    g = pl.program_id(1)
    rows = nb * ls
    shift = (LANES - HEADS_PER_GROUP * g) % LANES
    dt = pltpu.roll(dt_ref[...], shift, axis=1)
    alog = pltpu.roll(alog_ref[...], shift, axis=1)
    dsk = pltpu.roll(dsk_ref[...], shift, axis=1)
    a = dt * (-jnp.exp(alog))

    row = lax.broadcasted_iota(jnp.int32, (rows, rows), 0)
    col = lax.broadcasted_iota(jnp.int32, (rows, rows), 1)
    same = (row // ls) == (col // ls)
    causal = jnp.logical_and(same, row >= col)
    hi = lax.Precision.HIGHEST
    acs = jnp.dot(jnp.where(causal, 1.0, 0.0), a, preferred_element_type=F32, precision=hi)
    tot = jnp.dot(jnp.where(same, 1.0, 0.0), a, preferred_element_type=F32, precision=hi)
    acs_t = acs.T
    coef = jnp.exp(tot - acs) * dt
    eacs = jnp.exp(acs)
    etot = jnp.exp(tot)

    bg = b_ref[...].astype(BF16)
    cg = c_ref[...].astype(BF16)
    xg = x_ref[...]
    cb = lax.dot_general(cg, bg, _NT, preferred_element_type=F32)

    seq_of_col = lax.broadcasted_iota(jnp.int32, (GROUP_DIM, rows), 1) // ls
    yoff_t = jnp.zeros((GROUP_DIM, rows), F32)
    for b in range(nb):
        yb = lax.dot_general(st_ref[b, 0].astype(BF16), cg, _NT, preferred_element_type=F32)
        yoff_t = jnp.where(seq_of_col == b, yb, yoff_t)
    yoff = yoff_t.T

    ys, xcs = [], []
    for r in range(HEADS_PER_GROUP):
        decay = jnp.exp(jnp.where(causal, acs[:, r:r + 1] - acs_t[r:r + 1, :], -jnp.inf))
        w = (cb * decay).astype(BF16)
        xr = xg[:, r * HEAD_DIM:(r + 1) * HEAD_DIM]
        yd = _dot(w, (xr * dt[:, r:r + 1]).astype(BF16))
        ys.append(yd + eacs[:, r:r + 1] * yoff[:, r * HEAD_DIM:(r + 1) * HEAD_DIM] + dsk[:, r:r + 1] * xr)
        xcs.append(xr * coef[:, r:r + 1])
    y_ref[...] = jnp.concatenate(ys, axis=1)
    xcoef_t = jnp.concatenate(xcs, axis=1).T

    for b in range(nb):
        upd = _dot(jnp.where(seq_of_col == b, xcoef_t, 0.0).astype(BF16), bg)
        scale = jnp.concatenate(
            [jnp.broadcast_to(etot[b * ls:b * ls + 1, r:r + 1], (HEAD_DIM, D_STATE)) for r in range(HEADS_PER_GROUP)],
            axis=0)
        so_ref[b, 0] = scale * st_ref[b, 0] + upd


def _ssd_step(xbc, dt, st, alog, dsk, *, nb, ls):
    t = xbc.shape[0]
    n_seq = st.shape[0]
    rows = nb * ls
    b_blk0 = D_INNER // D_STATE
    c_blk0 = b_blk0 + N_GROUPS
    return pl.pallas_call(
        functools.partial(_ssd_step_kernel, nb=nb, ls=ls),
        grid=(n_seq // nb, N_GROUPS),
        in_specs=[
            pl.BlockSpec((rows, GROUP_DIM), lambda i, g: (i, g)),
            pl.BlockSpec((rows, D_STATE), lambda i, g: (i, b_blk0 + g)),
            pl.BlockSpec((rows, D_STATE), lambda i, g: (i, c_blk0 + g)),
            pl.BlockSpec((rows, LANES), lambda i, g: (i, 0)),
            pl.BlockSpec((nb, 1, GROUP_DIM, D_STATE), lambda i, g: (i, g, 0, 0)),
            _const_spec((1, LANES)),
            _const_spec((1, LANES)),
        ],
        out_specs=[
            pl.BlockSpec((rows, GROUP_DIM), lambda i, g: (i, g)),
            pl.BlockSpec((nb, 1, GROUP_DIM, D_STATE), lambda i, g: (i, g, 0, 0)),
        ],
        out_shape=[
            jax.ShapeDtypeStruct((t, D_INNER), F32),
            jax.ShapeDtypeStruct((n_seq, N_GROUPS, GROUP_DIM, D_STATE), F32),
        ],
        compiler_params=_params(2),
        name="ssd_step",
    )(xbc, xbc, xbc, dt, st, alog, dsk)


def _ssd_post_kernel(y_ref, z_ref, h_ref, ng_ref, wo_ref, o_ref):
    acc = h_ref[...]
    for g in range(N_GROUPS):
        lo, hi = g * GROUP_DIM, (g + 1) * GROUP_DIM
        v = y_ref[:, lo:hi] * _silu(z_ref[:, lo:hi])
        vn = v * lax.rsqrt(jnp.mean(v * v, axis=-1, keepdims=True) + EPS) * ng_ref[:, lo:hi]
        acc = acc + _dot(vn.astype(BF16), wo_ref[lo:hi, :])
    o_ref[...] = acc


def _ssd_post(y, z, h, ng, wo, *, tm):
    t = y.shape[0]
    return pl.pallas_call(
        _ssd_post_kernel,
        grid=(t // tm,),
        in_specs=[
            pl.BlockSpec((tm, D_INNER), lambda i: (i, 0)),
            pl.BlockSpec((tm, D_INNER), lambda i: (i, 0)),
            pl.BlockSpec((tm, D_MODEL), lambda i: (i, 0)),
            _const_spec((1, D_INNER)),
            _const_spec((D_INNER, D_MODEL)),
        ],
        out_specs=pl.BlockSpec((tm, D_MODEL), lambda i: (i, 0)),
        out_shape=jax.ShapeDtypeStruct((t, D_MODEL), F32),
        compiler_params=_params(1),
        name="ssd_post",
    )(y, z, h, ng, wo)


def _trunk(h, cc_hist, sc_hist, ssm, w, *, tb, tl, tm, short):
    bsz, seq, _ = h.shape
    t = bsz * seq
    h1, new_cc = _conformer(h, cc_hist, w["gm0"], w["w_pw1"], w["b_pw1"], w["w_dw"], w["b_dw"], w["ln_g"], w["ln_b"],
                            w["w_pw2"], w["b_pw2"], tb=tb, tl=tl)
    h2 = _ffn(h1.reshape(t, D_MODEL), w["gf0"], w["wg0"], w["wu0"], w["wd0"], w["g_final"], tm=tm, final=False)
    z, xbc, dt, new_sc = _ssd_front(h2.reshape(bsz, seq, D_MODEL), sc_hist, w["gm1"], w["w_z"], w["w_xbc"], w["w_dt"],
                                    w["conv_w"], w["conv_b"], w["dt_bias"], tb=tb, tl=tl)
    if short:
        y, new_ss = _ssd_step(xbc.reshape(t, CONV_DIM), dt.reshape(t, LANES), ssm, w["a_log"], w["d_skip"],
                              nb=LANES // seq, ls=seq)
        h3 = _ssd_post(y, z.reshape(t, D_INNER), h2, w["norm_g"], w["w_out"], tm=tm)
    else:
        pad = (-seq) % LANES
        rows = lambda v: jnp.pad(v, ((0, 0), (0, pad), (0, 0))) if pad else v
        h3, new_ss = _ssd_scan(rows(xbc), rows(dt), rows(z), rows(h2.reshape(bsz, seq, D_MODEL)), ssm,
                               w["a_log"], w["d_skip_lanes"], w["norm_g"], w["w_out"], lc=LANES)
        h3 = h3[:, :seq].reshape(t, D_MODEL)
    out = _ffn(h3, w["gf1"], w["wg1"], w["wu1"], w["wd1"], w["g_final"], tm=tm, final=True)
    return out.reshape(bsz, seq, D_MODEL), new_cc, new_sc, new_ss


def _pad_lanes(v):
    return jnp.pad(v.astype(F32), (0, LANES - v.shape[0])).reshape(1, LANES)


def kernel(x_prompt, x_sample, cache_conv, state_ssd_conv, state_ssm, meta_tokens, norm_mix, norm_ffn, norm_final, cf_w_pw1, cf_b_pw1, cf_w_dw, cf_b_dw, cf_ln_g, cf_ln_b, cf_w_pw2, cf_b_pw2, ssd_w_in, ssd_conv_w, ssd_conv_b, ssd_dt_bias, ssd_a_log, ssd_d, ssd_norm_g, ssd_w_out, ffn_w_gate, ffn_w_up, ffn_w_down):
    row = lambda v: v.astype(F32).reshape(1, -1)
    w_in = ssd_w_in[0]
    w = {
        "gm0": row(norm_mix[0]), "gm1": row(norm_mix[1]),
        "gf0": row(norm_ffn[0]), "gf1": row(norm_ffn[1]), "g_final": row(norm_final),
        "w_pw1": cf_w_pw1[0].astype(BF16), "b_pw1": row(cf_b_pw1[0]),
        "w_dw": jnp.broadcast_to(cf_w_dw[0][:, None, :], (CONV_WIDTH, SUBLANES, D_MODEL)),
        "b_dw": row(cf_b_dw[0]), "ln_g": row(cf_ln_g[0]), "ln_b": row(cf_ln_b[0]),
        "w_pw2": cf_w_pw2[0].astype(BF16), "b_pw2": row(cf_b_pw2[0]),
        "w_z": w_in[:, :D_INNER].astype(BF16),
        "w_xbc": w_in[:, D_INNER:D_INNER + CONV_DIM].astype(BF16),
        "w_dt": jnp.pad(w_in[:, D_INNER + CONV_DIM:], ((0, 0), (0, LANES - N_HEADS))).astype(BF16),
        "conv_w": jnp.broadcast_to(ssd_conv_w[0][:, None, :], (SSD_CONV_WIDTH, SUBLANES, CONV_DIM)),
        "conv_b": row(ssd_conv_b[0]),
        "dt_bias": _pad_lanes(ssd_dt_bias[0]), "a_log": _pad_lanes(ssd_a_log[0]), "d_skip": _pad_lanes(ssd_d[0]),
        "d_skip_lanes": jnp.repeat(ssd_d[0].astype(F32), HEAD_DIM).reshape(1, D_INNER),
        "norm_g": row(ssd_norm_g[0]), "w_out": ssd_w_out[0].astype(BF16),
        "wg0": ffn_w_gate[0].astype(BF16), "wu0": ffn_w_up[0].astype(BF16), "wd0": ffn_w_down[0].astype(BF16),
        "wg1": ffn_w_gate[1].astype(BF16), "wu1": ffn_w_up[1].astype(BF16), "wd1": ffn_w_down[1].astype(BF16),
    }
    bp, seq, _ = x_prompt.shape
    bs, dseq, _ = x_sample.shape

    _, cc_m, sc_m, ss_m = _trunk(
        meta_tokens.astype(F32)[None], jnp.zeros((1, HIST_PAD, D_MODEL), F32),
        jnp.zeros((1, SSD_HIST_PAD, CONV_DIM), F32), jnp.zeros((1, N_GROUPS, GROUP_DIM, D_STATE), F32), w,
        tb=1, tl=N_META, tm=N_META, short=False)

    y_prompt, cc_p, sc_p, ss_p = _trunk(
        x_prompt, jnp.broadcast_to(cc_m, (bp, HIST_PAD, D_MODEL)),
        jnp.broadcast_to(sc_m, (bp, SSD_HIST_PAD, CONV_DIM)),
        jnp.broadcast_to(ss_m, (bp, N_GROUPS, GROUP_DIM, D_STATE)), w,
        tb=1, tl=256, tm=512, short=False)

    cc_in = jnp.pad(cache_conv[0], ((0, 0), (HIST_PAD - CONV_HIST, 0), (0, 0)))
    sc_in = jnp.pad(state_ssd_conv[0], ((0, 0), (SSD_HIST_PAD - SSD_CONV_HIST, 0), (0, 0)))
    y_sample, cc_s, sc_s, ss_s = _trunk(
        x_sample, cc_in, sc_in, state_ssm[0].reshape(bs, N_GROUPS, GROUP_DIM, D_STATE), w,
        tb=32, tl=dseq, tm=512, short=True)

    unpad_cc = lambda v: v[None, :, HIST_PAD - CONV_HIST:, :]
    unpad_sc = lambda v: v[None, :, SSD_HIST_PAD - SSD_CONV_HIST:, :]
    unpack_ss = lambda v: v.reshape(1, v.shape[0], N_HEADS, HEAD_DIM, D_STATE)
    return (y_prompt, y_sample, unpad_cc(cc_p), unpad_cc(cc_s), unpad_sc(sc_p), unpad_sc(sc_s),
            unpack_ss(ss_p), unpack_ss(ss_s))
```

```python
import functools

import jax
import jax.numpy as jnp
from jax import lax
from jax.experimental import pallas as pl
from jax.experimental.pallas import tpu as pltpu

D_MODEL = 1024
N_META = 16
CONV_WIDTH = 31
CONV_HIST = CONV_WIDTH - 1
D_INNER = 2048
HEAD_DIM = 64
N_HEADS = 32
N_GROUPS = 8
HEADS_PER_GROUP = 4
GROUP_DIM = HEADS_PER_GROUP * HEAD_DIM
D_STATE = 128
SSD_CONV_WIDTH = 4
SSD_CONV_HIST = SSD_CONV_WIDTH - 1
CONV_DIM = D_INNER + 2 * N_GROUPS * D_STATE
D_FF = 2816
EPS = 1e-6

LANES = 128
SUBLANES = 8
HIST_PAD = 32
SSD_HIST_PAD = 8
FRONT_COLS = 256
VMEM_LIMIT = 56 * 1024 * 1024

F32 = jnp.float32
BF16 = jnp.bfloat16
_NT = (((1,), (1,)), ((), ()))


def _const_spec(shape):
    return pl.BlockSpec(shape, lambda *_: (0,) * len(shape), pipeline_mode=pl.Buffered(1))


def _params(n_axes):
    return pltpu.CompilerParams(dimension_semantics=("arbitrary",) * n_axes, vmem_limit_bytes=VMEM_LIMIT)


def _rms(x, g):
    return x * lax.rsqrt(jnp.mean(x * x, axis=-1, keepdims=True) + EPS) * g


def _sigmoid(x):
    return 1.0 / (1.0 + jnp.exp(-x))


def _silu(x):
    return x * _sigmoid(x)


def _dot(a, b):
    return jnp.dot(a, b, preferred_element_type=F32)


def _roll_groups_down(x, d):
    b, n, c = x.shape
    return pltpu.roll(x.reshape(b * n // SUBLANES, SUBLANES, c), d, axis=1).reshape(b, n, c)


def _shift_rows_up(x, s):
    n = x.shape[0] - SUBLANES
    rot = [pltpu.roll(x[j:j + SUBLANES], SUBLANES - s, axis=0) for j in range(0, n + SUBLANES, SUBLANES)]
    keep = lax.broadcasted_iota(jnp.int32, (SUBLANES, x.shape[1]), 0) < SUBLANES - s
    return jnp.concatenate([jnp.where(keep, rot[j], rot[j + 1]) for j in range(n // SUBLANES)], axis=0)


def _dwconv(src_ref, w_ref, bias_ref, dst_ref, *, tb, tl, rc, cc, taps, base, act, cols=None, static=False):
    n_rc = tl // rc
    c_lo, c_hi = cols if cols is not None else (0, src_ref.shape[-1])
    by_shift = {}
    for k in range(taps):
        by_shift.setdefault((base + k) % SUBLANES, []).append(k)

    def chunk(b, r0):
        for c0 in range(c_lo, c_hi, cc):
            acc = jnp.broadcast_to(bias_ref[:, c0:c0 + cc], (rc, cc))
            for s, ks in sorted(by_shift.items()):
                n_rows = rc + SUBLANES if s else rc
                part = None
                for k in ks:
                    q = (base + k) // SUBLANES
                    blk = src_ref[b, pl.ds(r0 + SUBLANES * q, n_rows), c0:c0 + cc]
                    wk = jnp.tile(w_ref[k, :, c0:c0 + cc], (n_rows // SUBLANES, 1))
                    part = blk * wk if part is None else part + blk * wk
                acc = acc + (_shift_rows_up(part, s) if s else part)
            dst_ref[b, pl.ds(r0, rc), c0:c0 + cc] = act(acc)

    if static:
        for b in range(tb):
            for j in range(n_rc):
                chunk(b, j * rc)
    else:
        def body(i, carry):
            chunk(i // n_rc, pl.multiple_of((i % n_rc) * rc, SUBLANES))
            return carry

        lax.fori_loop(0, tb * n_rc, body, 0)


def _conformer_kernel(h_ref, hist_ref, gm_ref, w1_ref, b1_ref, wdw_ref, bdw_ref, lng_ref, lnb_ref, w2_ref, b2_ref,
                      out_ref, nh_ref, gh_ref, c_ref, *, tb, tl, rc):
    l = pl.program_id(1)
    m = tb * tl

    @pl.when(l == 0)
    def _():
        gh_ref[:, 0:HIST_PAD - CONV_HIST, :] = jnp.zeros((tb, HIST_PAD - CONV_HIST, D_MODEL), F32)
        gh_ref[:, HIST_PAD - CONV_HIST:HIST_PAD, :] = hist_ref[...]

    x = h_ref[...].reshape(m, D_MODEL)
    hn = _rms(x, gm_ref[...]).astype(BF16)
    u = _dot(hn, w1_ref[...]) + b1_ref[...]
    g = u[:, :D_MODEL] * _sigmoid(u[:, D_MODEL:])
    gh_ref[:, HIST_PAD:HIST_PAD + tl, :] = g.reshape(tb, tl, D_MODEL)

    _dwconv(gh_ref, wdw_ref, bdw_ref, c_ref, tb=tb, tl=tl, rc=rc, cc=LANES if rc >= 64 else 4 * LANES,
            taps=CONV_WIDTH, base=HIST_PAD - CONV_HIST, act=lambda v: v)

    @pl.when(l == pl.num_programs(1) - 1)
    def _():
        nh_ref[...] = gh_ref[:, tl + HIST_PAD - CONV_HIST:tl + HIST_PAD, :]

    gh_ref[:, 0:HIST_PAD, :] = gh_ref[:, tl:tl + HIST_PAD, :]

    c = c_ref[...].reshape(m, D_MODEL)
    mu = jnp.mean(c, axis=-1, keepdims=True)
    xc = c - mu
    cn = xc * lax.rsqrt(jnp.mean(xc * xc, axis=-1, keepdims=True) + EPS) * lng_ref[...] + lnb_ref[...]
    cn = _silu(cn).astype(BF16)
    y = _dot(cn, w2_ref[...]) + b2_ref[...] + x
    out_ref[...] = y.reshape(tb, tl, D_MODEL)


def _conformer(h, hist, gm, w1, b1, wdw, bdw, lng, lnb, w2, b2, *, tb, tl):
    bsz, seq, _ = h.shape
    rc = 64 if tl % 64 == 0 else (16 if tl % 16 == 0 else SUBLANES)
    kern = functools.partial(_conformer_kernel, tb=tb, tl=tl, rc=rc)
    return pl.pallas_call(
        kern,
        grid=(bsz // tb, seq // tl),
        in_specs=[
            pl.BlockSpec((tb, tl, D_MODEL), lambda b, l: (b, l, 0)),
            pl.BlockSpec((tb, CONV_HIST, D_MODEL), lambda b, l: (b, 0, 0)),
            _const_spec((1, D_MODEL)),
            _const_spec((D_MODEL, 2 * D_MODEL)),
            _const_spec((1, 2 * D_MODEL)),
            _const_spec((CONV_WIDTH, SUBLANES, D_MODEL)),
            _const_spec((1, D_MODEL)),
            _const_spec((1, D_MODEL)),
            _const_spec((1, D_MODEL)),
            _const_spec((D_MODEL, D_MODEL)),
            _const_spec((1, D_MODEL)),
        ],
        out_specs=[
            pl.BlockSpec((tb, tl, D_MODEL), lambda b, l: (b, l, 0)),
            pl.BlockSpec((tb, CONV_HIST, D_MODEL), lambda b, l: (b, 0, 0)),
        ],
        out_shape=[
            jax.ShapeDtypeStruct((bsz, seq, D_MODEL), F32),
            jax.ShapeDtypeStruct((bsz, CONV_HIST, D_MODEL), F32),
        ],
        scratch_shapes=[
            pltpu.VMEM((tb, HIST_PAD + tl, D_MODEL), F32),
            pltpu.VMEM((tb, tl, D_MODEL), F32),
        ],
        compiler_params=_params(2),
        name="conformer",
    )(h, hist, gm, w1, b1, wdw, bdw, lng, lnb, w2, b2)


def _ffn_kernel(x_ref, g_ref, wg_ref, wu_ref, wd_ref, gf_ref, o_ref, *, final):
    x = x_ref[...]
    hn = _rms(x, g_ref[...]).astype(BF16)
    a = _dot(hn, wg_ref[...])
    b = _dot(hn, wu_ref[...])
    t = (_silu(a) * b).astype(BF16)
    y = x + _dot(t, wd_ref[...])
    if final:
        y = _rms(y, gf_ref[...])
    o_ref[...] = y


def _ffn(x, g, wg, wu, wd, gf, *, tm, final):
    t = x.shape[0]
    return pl.pallas_call(
        functools.partial(_ffn_kernel, final=final),
        grid=(t // tm,),
        in_specs=[
            pl.BlockSpec((tm, D_MODEL), lambda i: (i, 0)),
            _const_spec((1, D_MODEL)),
            _const_spec((D_MODEL, D_FF)),
            _const_spec((D_MODEL, D_FF)),
            _const_spec((D_FF, D_MODEL)),
            _const_spec((1, D_MODEL)),
        ],
        out_specs=pl.BlockSpec((tm, D_MODEL), lambda i: (i, 0)),
        out_shape=jax.ShapeDtypeStruct((t, D_MODEL), F32),
        compiler_params=_params(1),
        name="ffn_final" if final else "ffn",
    )(x, g, wg, wu, wd, gf)


def _softplus(x):
    return jnp.maximum(x, 0.0) + jnp.log(1.0 + jnp.exp(-jnp.abs(x)))


def _ssd_front_kernel(h_ref, ch_ref, gm_ref, wz_ref, wxa_ref, wxb_ref, wdt_ref, cw_ref, cb_ref, dtb_ref,
                      z_ref, xbc_ref, dt_ref, nc_ref, hist_ref, *, tb, tl):
    l = pl.program_id(1)
    m = tb * tl

    @pl.when(l == 0)
    def _():
        hist_ref[:, 0:SSD_HIST_PAD - SSD_CONV_HIST, :] = jnp.zeros((tb, SSD_HIST_PAD - SSD_CONV_HIST, CONV_DIM), F32)
        hist_ref[:, SSD_HIST_PAD - SSD_CONV_HIST:, :] = ch_ref[...]

    x = h_ref[...].reshape(m, D_MODEL)
    hn = _rms(x, gm_ref[...]).astype(BF16)
    dt_ref[...] = _softplus(_dot(hn, wdt_ref[...]) + dtb_ref[...]).reshape(tb, tl, LANES)
    row_in_group = lax.broadcasted_iota(jnp.int32, (tb, tl, FRONT_COLS), 1) % SUBLANES

    def conv_block(xn, lo):
        cs = slice(lo, lo + FRONT_COLS)
        xh = jnp.concatenate([hist_ref[:, :, cs], xn], axis=1)
        tap = lambda k: jnp.tile(cw_ref[k, :, cs], (tl // SUBLANES, 1))[None]
        acc = cb_ref[:, cs][None] + xn * tap(SSD_CONV_WIDTH - 1)
        for d in range(1, SSD_CONV_WIDTH):
            rot = _roll_groups_down(xh, d)
            acc = acc + jnp.where(row_in_group < d, rot[:, :tl], rot[:, SUBLANES:]) * tap(SSD_CONV_WIDTH - 1 - d)
        xbc_ref[:, :, cs] = _silu(acc)
        hist_ref[:, :, cs] = xn[:, tl - SSD_HIST_PAD:]

    def z_block(zn, lo):
        z_ref[:, :, lo:lo + FRONT_COLS] = zn

    tasks = []
    z_cols = list(range(0, D_INNER, FRONT_COLS))
    for i, lo in enumerate(range(0, CONV_DIM, FRONT_COLS)):
        w_ref, off = (wxa_ref, lo) if lo < D_INNER else (wxb_ref, lo - D_INNER)
        tasks.append((w_ref, off, conv_block, lo))
        if i % 2 == 1:
            tasks.append((wz_ref, z_cols[i // 2], z_block, z_cols[i // 2]))
    pending = None
    for w_ref, off, finish, lo in tasks:
        res = _dot(hn, w_ref[:, off:off + FRONT_COLS]).reshape(tb, tl, FRONT_COLS)
        if pending is not None:
            pending[0](pending[1], pending[2])
        pending = (finish, res, lo)
    pending[0](pending[1], pending[2])

    @pl.when(l == pl.num_programs(1) - 1)
    def _():
        nc_ref[...] = hist_ref[:, SSD_HIST_PAD - SSD_CONV_HIST:, :]


def _ssd_front(h, ch, gm, w_in, wdt, cw, cb, dtb, *, tb, tl):
    bsz, seq, _ = h.shape
    w_blk = lambda j: pl.BlockSpec((D_MODEL, D_INNER), lambda *_: (0, j), pipeline_mode=pl.Buffered(1))
    return pl.pallas_call(
        functools.partial(_ssd_front_kernel, tb=tb, tl=tl),
        grid=(bsz // tb, seq // tl),
        in_specs=[
            pl.BlockSpec((tb, tl, D_MODEL), lambda b, l: (b, l, 0)),
            pl.BlockSpec((tb, SSD_CONV_HIST, CONV_DIM), lambda b, l: (b, 0, 0)),
            _const_spec((1, D_MODEL)),
            w_blk(0),
            w_blk(1),
            w_blk(2),
            _const_spec((D_MODEL, LANES)),
            _const_spec((SSD_CONV_WIDTH, SUBLANES, CONV_DIM)),
            _const_spec((1, CONV_DIM)),
            _const_spec((1, LANES)),
        ],
        out_specs=[
            pl.BlockSpec((tb, tl, D_INNER), lambda b, l: (b, l, 0)),
            pl.BlockSpec((tb, tl, CONV_DIM), lambda b, l: (b, l, 0)),
            pl.BlockSpec((tb, tl, LANES), lambda b, l: (b, l, 0)),
            pl.BlockSpec((tb, SSD_CONV_HIST, CONV_DIM), lambda b, l: (b, 0, 0)),
        ],
        out_shape=[
            jax.ShapeDtypeStruct((bsz, seq, D_INNER), F32),
            jax.ShapeDtypeStruct((bsz, seq, CONV_DIM), F32),
            jax.ShapeDtypeStruct((bsz, seq, LANES), F32),
            jax.ShapeDtypeStruct((bsz, SSD_CONV_HIST, CONV_DIM), F32),
        ],
        scratch_shapes=[pltpu.VMEM((tb, SSD_HIST_PAD, CONV_DIM), F32)],
        compiler_params=_params(2),
        name="ssd_front",
    )(h, ch, gm, w_in, w_in, w_in, wdt, cw, cb, dtb)


def _cumsum_rows(a, tri):
    a_hi = a.astype(BF16)
    r1 = a - a_hi.astype(F32)
    a_mid = r1.astype(BF16)
    a_lo = (r1 - a_mid.astype(F32)).astype(BF16)
    return _dot(tri, a_hi) + _dot(tri, a_mid) + _dot(tri, a_lo)


def _head_rows(mat, g, width):
    return jnp.concatenate(
        [jnp.broadcast_to(mat[HEADS_PER_GROUP * g + r:HEADS_PER_GROUP * g + r + 1, :], (HEAD_DIM, width))
         for r in range(HEADS_PER_GROUP)], axis=0)


def _ssd_scan_kernel(xbc_ref, dt_ref, dtn_ref, z_ref, h_ref, h0_ref, alog_ref, dske_ref, ng_ref, wo_ref, o_ref, st_ref,
                     vn_ref, dec_ref, *, lc, nb):
    c = pl.program_id(1)
    slot = c % 2

    row = lax.broadcasted_iota(jnp.int32, (lc, lc), 0)
    col = lax.broadcasted_iota(jnp.int32, (lc, lc), 1)
    causal = row >= col
    tri = jnp.where(causal, 1.0, 0.0).astype(BF16)
    lane_head = lax.broadcasted_iota(jnp.int32, (lc, GROUP_DIM), 1) // HEAD_DIM
    lane_lo = lax.broadcasted_iota(jnp.int32, (lc, LANES), 1) < HEAD_DIM
    neg_a = -jnp.exp(alog_ref[...])

    def group_dots(i, g):
        b_off = D_INNER + g * D_STATE
        c_off = D_INNER + N_GROUPS * D_STATE + g * D_STATE
        bg = xbc_ref[i, :, b_off:b_off + D_STATE].astype(BF16)
        cg = xbc_ref[i, :, c_off:c_off + D_STATE].astype(BF16)
        cb = lax.dot_general(cg, bg, _NT, preferred_element_type=F32)
        hg = st_ref[i, g]
        yoff = lax.dot_general(cg, hg.astype(BF16), _NT, preferred_element_type=F32)
        return bg, cb, hg, yoff

    def decays(ref, i):
        dt = ref[i]
        acs = _cumsum_rows(dt * neg_a, tri)
        acs_t = acs.T
        dt_t = dt.T
        last_t = jnp.broadcast_to(acs_t[:, lc - 1:lc], (LANES, lc))
        c_t = acs_t - jnp.log(dt_t)
        coef_t = jnp.exp(last_t - acs_t) * dt_t
        cd = jnp.broadcast_to(jnp.exp(acs_t[:, lc - 1:lc]), (LANES, D_STATE))
        return acs, c_t, coef_t, cd

    def group_tail(i, g, dots, dec):
        bg, cb, hg, yoff = dots
        acs, c_t, coef_t, cd = dec
        lo, hi = g * GROUP_DIM, (g + 1) * GROUP_DIM
        xg = xbc_ref[i, :, lo:hi]
        ws, acs_b = [], []
        for r in range(HEADS_PER_GROUP):
            h = HEADS_PER_GROUP * g + r
            ab = jnp.broadcast_to(acs[:, h:h + 1], (lc, LANES))
            acs_b.append(ab)
            ws.append((cb * jnp.exp(jnp.where(causal, ab - c_t[h:h + 1, :], -jnp.inf))).astype(BF16))
        x_bd = jnp.concatenate([jnp.where(lane_head == r, xg, 0.0).astype(BF16) for r in range(HEADS_PER_GROUP)],
                               axis=0)
        yd = _dot(jnp.concatenate(ws, axis=1), x_bd)
        e_b = jnp.exp(jnp.concatenate([jnp.where(lane_lo, acs_b[0], acs_b[1]),
                                       jnp.where(lane_lo, acs_b[2], acs_b[3])], axis=1))
        y = yd + e_b * yoff + dske_ref[:, lo:hi] * xg

        xcoef_t = (xg.T * _head_rows(coef_t, g, lc)).astype(BF16)
        st_ref[i, g] = _head_rows(cd, g, D_STATE) * hg + _dot(xcoef_t, bg)

        v = y * _silu(z_ref[i, :, lo:hi])
        vn = v * lax.rsqrt(jnp.mean(v * v, axis=-1, keepdims=True) + EPS) * ng_ref[:, lo:hi]
        vn_ref[i, :, lo:hi] = vn.astype(BF16)

    @pl.when(c == 0)
    def _():
        st_ref[...] = h0_ref[...]
        for i in range(nb):
            for k, v in enumerate(decays(dt_ref, i)):
                dec_ref[0, i, k] = v

    half = D_INNER // 2
    dots = [group_dots(i, 0) for i in range(nb)]
    dec = [tuple(dec_ref[slot, i, k] for k in range(4)) for i in range(nb)]
    for g in range(N_GROUPS):
        nxt = [group_dots(i, g + 1) for i in range(nb)] if g + 1 < N_GROUPS else None
        for i in range(nb):
            group_tail(i, g, dots[i], dec[i])
        dots = nxt
        if g == 1:
            for i in range(nb):
                for k, v in enumerate(decays(dtn_ref, i)):
                    dec_ref[1 - slot, i, k] = v
        if g == N_GROUPS // 2 - 1:
            for i in range(nb):
                o_ref[i] = h_ref[i] + _dot(vn_ref[i, :, :half], wo_ref[:half, :])
    for i in range(nb):
        o_ref[i] = o_ref[i] + _dot(vn_ref[i, :, half:], wo_ref[half:, :])


def _ssd_scan(xbc, dt, z, h, h0, alog, dske, ng, wo, *, lc):
    bsz, seq, _ = xbc.shape
    assert lc == LANES
    nb = 1
    last = seq // lc - 1
    return pl.pallas_call(
        functools.partial(_ssd_scan_kernel, lc=lc, nb=nb),
        grid=(bsz // nb, seq // lc),
        in_specs=[
            pl.BlockSpec((nb, lc, CONV_DIM), lambda b, c: (b, c, 0)),
            pl.BlockSpec((nb, lc, LANES), lambda b, c: (b, c, 0)),
            pl.BlockSpec((nb, lc, LANES), lambda b, c: (b, jnp.minimum(c + 1, last), 0)),
            pl.BlockSpec((nb, lc, D_INNER), lambda b, c: (b, c, 0)),
            pl.BlockSpec((nb, lc, D_MODEL), lambda b, c: (b, c, 0)),
            pl.BlockSpec((nb, N_GROUPS, GROUP_DIM, D_STATE), lambda b, c: (b, 0, 0, 0)),
            _const_spec((1, LANES)),
            _const_spec((1, D_INNER)),
            _const_spec((1, D_INNER)),
            _const_spec((D_INNER, D_MODEL)),
        ],
        out_specs=[
            pl.BlockSpec((nb, lc, D_MODEL), lambda b, c: (b, c, 0)),
            pl.BlockSpec((nb, N_GROUPS, GROUP_DIM, D_STATE), lambda b, c: (b, 0, 0, 0)),
        ],
        out_shape=[
            jax.ShapeDtypeStruct((bsz, seq, D_MODEL), F32),
            jax.ShapeDtypeStruct((bsz, N_GROUPS, GROUP_DIM, D_STATE), F32),
        ],
        scratch_shapes=[pltpu.VMEM((nb, lc, D_INNER), BF16), pltpu.VMEM((2, nb, 4, LANES, LANES), F32)],
        compiler_params=_params(2),
        name="ssd_scan",
    )(xbc, dt, dt, z, h, h0, alog, dske, ng, wo)


def _ssd_step_kernel(x_ref, b_ref, c_ref, dt_ref, st_ref, alog_ref, dsk_ref, y_ref, so_ref, *, nb, ls):
    g = pl.program_id(1)
    rows = nb * ls
    shift = (LANES - HEADS_PER_GROUP * g) % LANES
    dt = pltpu.roll(dt_ref[...], shift, axis=1)
    alog = pltpu.roll(alog_ref[...], shift, axis=1)
    dsk = pltpu.roll(dsk_ref[...], shift, axis=1)
    a = dt * (-jnp.exp(alog))

    row = lax.broadcasted_iota(jnp.int32, (rows, rows), 0)
    col = lax.broadcasted_iota(jnp.int32, (rows, rows), 1)
    same = (row // ls) == (col // ls)
    causal = jnp.logical_and(same, row >= col)
    hi = lax.Precision.HIGHEST
    acs = jnp.dot(jnp.where(causal, 1.0, 0.0), a, preferred_element_type=F32, precision=hi)
    tot = jnp.dot(jnp.where(same, 1.0, 0.0), a, preferred_element_type=F32, precision=hi)
    acs_t = acs.T
    coef = jnp.exp(tot - acs) * dt
    eacs = jnp.exp(acs)
    etot = jnp.exp(tot)

    bg = b_ref[...].astype(BF16)
    cg = c_ref[...].astype(BF16)
    xg = x_ref[...]
    cb = lax.dot_general(cg, bg, _NT, preferred_element_type=F32)

    seq_of_col = lax.broadcasted_iota(jnp.int32, (GROUP_DIM, rows), 1) // ls
    yoff_t = jnp.zeros((GROUP_DIM, rows), F32)
    for b in range(nb):
        yb = lax.dot_general(st_ref[b, 0].astype(BF16), cg, _NT, preferred_element_type=F32)
        yoff_t = jnp.where(seq_of_col == b, yb, yoff_t)
    yoff = yoff_t.T

    ys, xcs = [], []
    for r in range(HEADS_PER_GROUP):
        decay = jnp.exp(jnp.where(causal, acs[:, r:r + 1] - acs_t[r:r + 1, :], -jnp.inf))
        w = (cb * decay).astype(BF16)
        xr = xg[:, r * HEAD_DIM:(r + 1) * HEAD_DIM]
        yd = _dot(w, (xr * dt[:, r:r + 1]).astype(BF16))
        ys.append(yd + eacs[:, r:r + 1] * yoff[:, r * HEAD_DIM:(r + 1) * HEAD_DIM] + dsk[:, r:r + 1] * xr)
        xcs.append(xr * coef[:, r:r + 1])
    y_ref[...] = jnp.concatenate(ys, axis=1)
    xcoef_t = jnp.concatenate(xcs, axis=1).T

    for b in range(nb):
        upd = _dot(jnp.where(seq_of_col == b, xcoef_t, 0.0).astype(BF16), bg)
        scale = jnp.concatenate(
            [jnp.broadcast_to(etot[b * ls:b * ls + 1, r:r + 1], (HEAD_DIM, D_STATE)) for r in range(HEADS_PER_GROUP)],
            axis=0)
        so_ref[b, 0] = scale * st_ref[b, 0] + upd


def _ssd_step(xbc, dt, st, alog, dsk, *, nb, ls):
    t = xbc.shape[0]
    n_seq = st.shape[0]
    rows = nb * ls
    b_blk0 = D_INNER // D_STATE
    c_blk0 = b_blk0 + N_GROUPS
    return pl.pallas_call(
        functools.partial(_ssd_step_kernel, nb=nb, ls=ls),
        grid=(n_seq // nb, N_GROUPS),
        in_specs=[
            pl.BlockSpec((rows, GROUP_DIM), lambda i, g: (i, g)),
            pl.BlockSpec((rows, D_STATE), lambda i, g: (i, b_blk0 + g)),
            pl.BlockSpec((rows, D_STATE), lambda i, g: (i, c_blk0 + g)),
            pl.BlockSpec((rows, LANES), lambda i, g: (i, 0)),
            pl.BlockSpec((nb, 1, GROUP_DIM, D_STATE), lambda i, g: (i, g, 0, 0)),
            _const_spec((1, LANES)),
            _const_spec((1, LANES)),
        ],
        out_specs=[
            pl.BlockSpec((rows, GROUP_DIM), lambda i, g: (i, g)),
            pl.BlockSpec((nb, 1, GROUP_DIM, D_STATE), lambda i, g: (i, g, 0, 0)),
        ],
        out_shape=[
            jax.ShapeDtypeStruct((t, D_INNER), F32),
            jax.ShapeDtypeStruct((n_seq, N_GROUPS, GROUP_DIM, D_STATE), F32),
        ],
        compiler_params=_params(2),
        name="ssd_step",
    )(xbc, xbc, xbc, dt, st, alog, dsk)


def _ssd_post_kernel(y_ref, z_ref, h_ref, ng_ref, wo_ref, o_ref):
    acc = h_ref[...]
    for g in range(N_GROUPS):
        lo, hi = g * GROUP_DIM, (g + 1) * GROUP_DIM
        v = y_ref[:, lo:hi] * _silu(z_ref[:, lo:hi])
        vn = v * lax.rsqrt(jnp.mean(v * v, axis=-1, keepdims=True) + EPS) * ng_ref[:, lo:hi]
        acc = acc + _dot(vn.astype(BF16), wo_ref[lo:hi, :])
    o_ref[...] = acc


def _ssd_post(y, z, h, ng, wo, *, tm):
    t = y.shape[0]
    return pl.pallas_call(
        _ssd_post_kernel,
        grid=(t // tm,),
        in_specs=[
            pl.BlockSpec((tm, D_INNER), lambda i: (i, 0)),
            pl.BlockSpec((tm, D_INNER), lambda i: (i, 0)),
            pl.BlockSpec((tm, D_MODEL), lambda i: (i, 0)),
            _const_spec((1, D_INNER)),
            _const_spec((D_INNER, D_MODEL)),
        ],
        out_specs=pl.BlockSpec((tm, D_MODEL), lambda i: (i, 0)),
        out_shape=jax.ShapeDtypeStruct((t, D_MODEL), F32),
        compiler_params=_params(1),
        name="ssd_post",
    )(y, z, h, ng, wo)


def _trunk(h, cc_hist, sc_hist, ssm, w, *, tb, tl, tm, short):
    bsz, seq, _ = h.shape
    t = bsz * seq
    h1, new_cc = _conformer(h, cc_hist, w["gm0"], w["w_pw1"], w["b_pw1"], w["w_dw"], w["b_dw"], w["ln_g"], w["ln_b"],
                            w["w_pw2"], w["b_pw2"], tb=tb, tl=tl)
    h2 = _ffn(h1.reshape(t, D_MODEL), w["gf0"], w["wg0"], w["wu0"], w["wd0"], w["g_final"], tm=tm, final=False)
    z, xbc, dt, new_sc = _ssd_front(h2.reshape(bsz, seq, D_MODEL), sc_hist, w["gm1"], w["w_in"], w["w_dt"],
                                    w["conv_w"], w["conv_b"], w["dt_bias"], tb=tb, tl=tl)
    if short:
        y, new_ss = _ssd_step(xbc.reshape(t, CONV_DIM), dt.reshape(t, LANES), ssm, w["a_log"], w["d_skip"],
                              nb=LANES // seq, ls=seq)
        h3 = _ssd_post(y, z.reshape(t, D_INNER), h2, w["norm_g"], w["w_out"], tm=tm)
    else:
        pad = (-seq) % LANES
        rows = lambda v: jnp.pad(v, ((0, 0), (0, pad), (0, 0))) if pad else v
        h3, new_ss = _ssd_scan(rows(xbc), rows(dt), rows(z), rows(h2.reshape(bsz, seq, D_MODEL)), ssm,
                               w["a_log"], w["d_skip_lanes"], w["norm_g"], w["w_out"], lc=LANES)
        h3 = h3[:, :seq].reshape(t, D_MODEL)
    out = _ffn(h3, w["gf1"], w["wg1"], w["wu1"], w["wd1"], w["g_final"], tm=tm, final=True)
    return out.reshape(bsz, seq, D_MODEL), new_cc, new_sc, new_ss


def _pad_lanes(v):
    return jnp.pad(v.astype(F32), (0, LANES - v.shape[0])).reshape(1, LANES)


def kernel(x_prompt, x_sample, cache_conv, state_ssd_conv, state_ssm, meta_tokens, norm_mix, norm_ffn, norm_final, cf_w_pw1, cf_b_pw1, cf_w_dw, cf_b_dw, cf_ln_g, cf_ln_b, cf_w_pw2, cf_b_pw2, ssd_w_in, ssd_conv_w, ssd_conv_b, ssd_dt_bias, ssd_a_log, ssd_d, ssd_norm_g, ssd_w_out, ffn_w_gate, ffn_w_up, ffn_w_down):
    row = lambda v: v.astype(F32).reshape(1, -1)
    w_in = ssd_w_in[0]
    w = {
        "gm0": row(norm_mix[0]), "gm1": row(norm_mix[1]),
        "gf0": row(norm_ffn[0]), "gf1": row(norm_ffn[1]), "g_final": row(norm_final),
        "w_pw1": cf_w_pw1[0].astype(BF16), "b_pw1": row(cf_b_pw1[0]),
        "w_dw": jnp.broadcast_to(cf_w_dw[0][:, None, :], (CONV_WIDTH, SUBLANES, D_MODEL)),
        "b_dw": row(cf_b_dw[0]), "ln_g": row(cf_ln_g[0]), "ln_b": row(cf_ln_b[0]),
        "w_pw2": cf_w_pw2[0].astype(BF16), "b_pw2": row(cf_b_pw2[0]),
        "w_in": w_in.astype(BF16),
        "w_dt": jnp.pad(w_in[:, D_INNER + CONV_DIM:], ((0, 0), (0, LANES - N_HEADS))).astype(BF16),
        "conv_w": jnp.broadcast_to(ssd_conv_w[0][:, None, :], (SSD_CONV_WIDTH, SUBLANES, CONV_DIM)),
        "conv_b": row(ssd_conv_b[0]),
        "dt_bias": _pad_lanes(ssd_dt_bias[0]), "a_log": _pad_lanes(ssd_a_log[0]), "d_skip": _pad_lanes(ssd_d[0]),
        "d_skip_lanes": jnp.repeat(ssd_d[0].astype(F32), HEAD_DIM).reshape(1, D_INNER),
        "norm_g": row(ssd_norm_g[0]), "w_out": ssd_w_out[0].astype(BF16),
        "wg0": ffn_w_gate[0].astype(BF16), "wu0": ffn_w_up[0].astype(BF16), "wd0": ffn_w_down[0].astype(BF16),
        "wg1": ffn_w_gate[1].astype(BF16), "wu1": ffn_w_up[1].astype(BF16), "wd1": ffn_w_down[1].astype(BF16),
    }
    bp, seq, _ = x_prompt.shape
    bs, dseq, _ = x_sample.shape

    _, cc_m, sc_m, ss_m = _trunk(
        meta_tokens.astype(F32)[None], jnp.zeros((1, CONV_HIST, D_MODEL), F32),
        jnp.zeros((1, SSD_CONV_HIST, CONV_DIM), F32), jnp.zeros((1, N_GROUPS, GROUP_DIM, D_STATE), F32), w,
        tb=1, tl=N_META, tm=N_META, short=False)

    y_prompt, cc_p, sc_p, ss_p = _trunk(
        x_prompt, jnp.broadcast_to(cc_m, (bp, CONV_HIST, D_MODEL)),
        jnp.broadcast_to(sc_m, (bp, SSD_CONV_HIST, CONV_DIM)),
        jnp.broadcast_to(ss_m, (bp, N_GROUPS, GROUP_DIM, D_STATE)), w,
        tb=1, tl=256, tm=512, short=False)

    y_sample, cc_s, sc_s, ss_s = _trunk(
        x_sample, cache_conv[0], state_ssd_conv[0], state_ssm[0].reshape(bs, N_GROUPS, GROUP_DIM, D_STATE), w,
        tb=32, tl=dseq, tm=512, short=True)

    unpack_ss = lambda v: v.reshape(1, v.shape[0], N_HEADS, HEAD_DIM, D_STATE)
    return (y_prompt, y_sample, cc_p[None], cc_s[None], sc_p[None], sc_s[None], unpack_ss(ss_p), unpack_ss(ss_s))
```

```python
import functools

import jax
import jax.numpy as jnp
from jax import lax
from jax.experimental import pallas as pl
from jax.experimental.pallas import tpu as pltpu

D_MODEL = 1024
N_META = 16
CONV_WIDTH = 31
CONV_HIST = CONV_WIDTH - 1
D_INNER = 2048
HEAD_DIM = 64
N_HEADS = 32
N_GROUPS = 8
HEADS_PER_GROUP = 4
GROUP_DIM = HEADS_PER_GROUP * HEAD_DIM
D_STATE = 128
SSD_CONV_WIDTH = 4
SSD_CONV_HIST = SSD_CONV_WIDTH - 1
CONV_DIM = D_INNER + 2 * N_GROUPS * D_STATE
D_FF = 2816
EPS = 1e-6

LANES = 128
SUBLANES = 8
HIST_PAD = 32
SSD_HIST_PAD = 8
FRONT_COLS = 256
FFN_COLS = 512
VMEM_LIMIT = 56 * 1024 * 1024

F32 = jnp.float32
BF16 = jnp.bfloat16
_NT = (((1,), (1,)), ((), ()))


def _const_spec(shape):
    return pl.BlockSpec(shape, lambda *_: (0,) * len(shape), pipeline_mode=pl.Buffered(1))


def _params(n_axes):
    return pltpu.CompilerParams(dimension_semantics=("arbitrary",) * n_axes, vmem_limit_bytes=VMEM_LIMIT)


def _rms(x, g):
    return x * lax.rsqrt(jnp.mean(x * x, axis=-1, keepdims=True) + EPS) * g


def _sigmoid(x):
    return 1.0 / (1.0 + jnp.exp(-x))


def _silu(x):
    return x * _sigmoid(x)


def _dot(a, b):
    return jnp.dot(a, b, preferred_element_type=F32)


def _roll_groups_down(x, d):
    b, n, c = x.shape
    return pltpu.roll(x.reshape(b * n // SUBLANES, SUBLANES, c), d, axis=1).reshape(b, n, c)


def _shift_rows_up(x, s):
    n = x.shape[0] - SUBLANES
    rot = [pltpu.roll(x[j:j + SUBLANES], SUBLANES - s, axis=0) for j in range(0, n + SUBLANES, SUBLANES)]
    keep = lax.broadcasted_iota(jnp.int32, (SUBLANES, x.shape[1]), 0) < SUBLANES - s
    return jnp.concatenate([jnp.where(keep, rot[j], rot[j + 1]) for j in range(n // SUBLANES)], axis=0)


def _dwconv(src_ref, w_ref, bias_ref, dst_ref, *, tb, tl, rc, cc, taps, base, act, cols=None, static=False):
    n_rc = tl // rc
    c_lo, c_hi = cols if cols is not None else (0, src_ref.shape[-1])
    by_shift = {}
    for k in range(taps):
        by_shift.setdefault((base + k) % SUBLANES, []).append(k)

    def piece(b, r0, c0):
        acc = jnp.broadcast_to(bias_ref[:, c0:c0 + cc], (rc, cc))
        for s, ks in sorted(by_shift.items()):
            n_rows = rc + SUBLANES if s else rc
            part = None
            for k in ks:
                q = (base + k) // SUBLANES
                blk = src_ref[b, pl.ds(r0 + SUBLANES * q, n_rows), c0:c0 + cc]
                wk = jnp.tile(w_ref[k, :, c0:c0 + cc], (n_rows // SUBLANES, 1))
                part = blk * wk if part is None else part + blk * wk
            acc = acc + (_shift_rows_up(part, s) if s else part)
        dst_ref[b, pl.ds(r0, rc), c0:c0 + cc] = act(acc)

    def chunk(b, r0):
        for c0 in range(c_lo, c_hi, cc):
            piece(b, r0, c0)

    if static == "deferred":
        return [functools.partial(piece, b, j * rc, c0)
                for b in range(tb) for j in range(n_rc) for c0 in range(c_lo, c_hi, cc)]
    if static:
        for b in range(tb):
            for j in range(n_rc):
                chunk(b, j * rc)
    else:
        def body(i, carry):
            chunk(i // n_rc, pl.multiple_of((i % n_rc) * rc, SUBLANES))
            return carry

        lax.fori_loop(0, tb * n_rc, body, 0)


def _conformer_kernel(h_ref, hist_ref, gm_ref, w1_ref, b1_ref, wdw_ref, bdw_ref, lng_ref, lnb_ref, w2_ref, b2_ref,
                      out_ref, nh_ref, gh_ref, c_ref, *, tb, tl, rc):
    l = pl.program_id(1)
    m = tb * tl

    @pl.when(l == 0)
    def _():
        gh_ref[:, 0:HIST_PAD - CONV_HIST, :] = jnp.zeros((tb, HIST_PAD - CONV_HIST, D_MODEL), F32)
        gh_ref[:, HIST_PAD - CONV_HIST:HIST_PAD, :] = hist_ref[...]

    x = h_ref[...].reshape(m, D_MODEL)
    hn = _rms(x, gm_ref[...]).astype(BF16)
    u = _dot(hn, w1_ref[...]) + b1_ref[...]
    g = u[:, :D_MODEL] * _sigmoid(u[:, D_MODEL:])
    gh_ref[:, HIST_PAD:HIST_PAD + tl, :] = g.reshape(tb, tl, D_MODEL)

    _dwconv(gh_ref, wdw_ref, bdw_ref, c_ref, tb=tb, tl=tl, rc=rc, cc=LANES if rc >= 64 else 4 * LANES,
            taps=CONV_WIDTH, base=HIST_PAD - CONV_HIST, act=lambda v: v)

    @pl.when(l == pl.num_programs(1) - 1)
    def _():
        nh_ref[...] = gh_ref[:, tl + HIST_PAD - CONV_HIST:tl + HIST_PAD, :]

    gh_ref[:, 0:HIST_PAD, :] = gh_ref[:, tl:tl + HIST_PAD, :]

    c = c_ref[...].reshape(m, D_MODEL)
    mu = jnp.mean(c, axis=-1, keepdims=True)
    xc = c - mu
    cn = xc * lax.rsqrt(jnp.mean(xc * xc, axis=-1, keepdims=True) + EPS) * lng_ref[...] + lnb_ref[...]
    cn = _silu(cn).astype(BF16)
    y = _dot(cn, w2_ref[...]) + b2_ref[...] + x
    out_ref[...] = y.reshape(tb, tl, D_MODEL)


def _conformer(h, hist, gm, w1, b1, wdw, bdw, lng, lnb, w2, b2, *, tb, tl):
    bsz, seq, _ = h.shape
    rc = 64 if tl % 64 == 0 else (16 if tl % 16 == 0 else SUBLANES)
    kern = functools.partial(_conformer_kernel, tb=tb, tl=tl, rc=rc)
    return pl.pallas_call(
        kern,
        grid=(bsz // tb, seq // tl),
        in_specs=[
            pl.BlockSpec((tb, tl, D_MODEL), lambda b, l: (b, l, 0)),
            pl.BlockSpec((tb, CONV_HIST, D_MODEL), lambda b, l: (b, 0, 0)),
            _const_spec((1, D_MODEL)),
            _const_spec((D_MODEL, 2 * D_MODEL)),
            _const_spec((1, 2 * D_MODEL)),
            _const_spec((CONV_WIDTH, SUBLANES, D_MODEL)),
            _const_spec((1, D_MODEL)),
            _const_spec((1, D_MODEL)),
            _const_spec((1, D_MODEL)),
            _const_spec((D_MODEL, D_MODEL)),
            _const_spec((1, D_MODEL)),
        ],
        out_specs=[
            pl.BlockSpec((tb, tl, D_MODEL), lambda b, l: (b, l, 0)),
            pl.BlockSpec((tb, CONV_HIST, D_MODEL), lambda b, l: (b, 0, 0)),
        ],
        out_shape=[
            jax.ShapeDtypeStruct((bsz, seq, D_MODEL), F32),
            jax.ShapeDtypeStruct((bsz, CONV_HIST, D_MODEL), F32),
        ],
        scratch_shapes=[
            pltpu.VMEM((tb, HIST_PAD + tl, D_MODEL), F32),
            pltpu.VMEM((tb, tl, D_MODEL), F32),
        ],
        compiler_params=_params(2),
        name="conformer",
    )(h, hist, gm, w1, b1, wdw, bdw, lng, lnb, w2, b2)


def _ffn_kernel(x_ref, g_ref, wg_ref, wu_ref, wd_ref, gf_ref, o_ref, *, final):
    x = x_ref[...]
    hn = _rms(x, g_ref[...]).astype(BF16)
    a = _dot(hn, wg_ref[...])
    b = _dot(hn, wu_ref[...])
    t = (_silu(a) * b).astype(BF16)
    y = x + _dot(t, wd_ref[...])
    if final:
        y = _rms(y, gf_ref[...])
    o_ref[...] = y


def _ffn(x, g, wg, wu, wd, gf, *, tm, final):
    t = x.shape[0]
    return pl.pallas_call(
        functools.partial(_ffn_kernel, final=final),
        grid=(t // tm,),
        in_specs=[
            pl.BlockSpec((tm, D_MODEL), lambda i: (i, 0)),
            _const_spec((1, D_MODEL)),
            _const_spec((D_MODEL, D_FF)),
            _const_spec((D_MODEL, D_FF)),
            _const_spec((D_FF, D_MODEL)),
            _const_spec((1, D_MODEL)),
        ],
        out_specs=pl.BlockSpec((tm, D_MODEL), lambda i: (i, 0)),
        out_shape=jax.ShapeDtypeStruct((t, D_MODEL), F32),
        compiler_params=_params(1),
        name="ffn_final" if final else "ffn",
    )(x, g, wg, wu, wd, gf)


def _layer0_kernel(h_ref, hist_ref, gm_ref, w1_ref, b1_ref, wdw_ref, bdw_ref, lng_ref, lnb_ref, w2_ref, b2_ref,
                   gf_ref, wg_ref, wu_ref, wd_ref, out_ref, nh_ref, gh_ref, c_ref, h1_ref, *, tl, rc, nl, nt):
    s = pl.program_id(0)
    slot = s % 2
    tile = jnp.minimum(s, nt - 1)

    @pl.when(s == 0)
    def _():
        h1_ref[1] = jnp.zeros((tl, D_MODEL), F32)

    @pl.when(tile % nl == 0)
    def _():
        gh_ref[:, 0:HIST_PAD - CONV_HIST, :] = jnp.zeros((1, HIST_PAD - CONV_HIST, D_MODEL), F32)
        gh_ref[:, HIST_PAD - CONV_HIST:HIST_PAD, :] = hist_ref[...]

    x1 = h1_ref[1 - slot]
    hn1 = _rms(x1, gf_ref[...]).astype(BF16)
    out_ref[0] = x1

    def ffn_block(lo, hi):
        t = (_silu(_dot(hn1, wg_ref[:, lo:hi])) * _dot(hn1, wu_ref[:, lo:hi])).astype(BF16)
        out_ref[0] = out_ref[0] + _dot(t, wd_ref[lo:hi, :])

    x = h_ref[0]
    hn = _rms(x, gm_ref[...]).astype(BF16)
    u = _dot(hn, w1_ref[...]) + b1_ref[...]
    gh_ref[0, HIST_PAD:HIST_PAD + tl, :] = u[:, :D_MODEL] * _sigmoid(u[:, D_MODEL:])
    pieces = _dwconv(gh_ref, wdw_ref, bdw_ref, c_ref, tb=1, tl=tl, rc=rc, cc=LANES, taps=CONV_WIDTH,
                     base=HIST_PAD - CONV_HIST, act=lambda v: v, static="deferred")
    blocks = [(lo, min(lo + FFN_COLS, D_FF)) for lo in range(0, D_FF, FFN_COLS)]
    per_block = -(-len(pieces) // len(blocks))
    for j, (lo, hi) in enumerate(blocks):
        @pl.when(s >= 0)
        def _(j=j, lo=lo, hi=hi):
            ffn_block(lo, hi)
            for p in pieces[j * per_block:(j + 1) * per_block]:
                p()
    nh_ref[...] = gh_ref[:, tl + HIST_PAD - CONV_HIST:tl + HIST_PAD, :]
    gh_ref[:, 0:HIST_PAD, :] = gh_ref[:, tl:tl + HIST_PAD, :]
    c = c_ref[0]
    xc = c - jnp.mean(c, axis=-1, keepdims=True)
    cn = xc * lax.rsqrt(jnp.mean(xc * xc, axis=-1, keepdims=True) + EPS) * lng_ref[...] + lnb_ref[...]
    h1_ref[slot] = _dot(_silu(cn).astype(BF16), w2_ref[...]) + b2_ref[...] + x


def _layer0(h, hist, w, *, tl):
    bsz, seq, _ = h.shape
    nl = seq // tl
    nt = bsz * nl
    cur = lambda s: jnp.minimum(s, nt - 1)
    prev = lambda s: jnp.maximum(s - 1, 0)
    return pl.pallas_call(
        functools.partial(_layer0_kernel, tl=tl, rc=64, nl=nl, nt=nt),
        grid=(nt + 1,),
        in_specs=[
            pl.BlockSpec((1, tl, D_MODEL), lambda s: (cur(s) // nl, cur(s) % nl, 0)),
            pl.BlockSpec((1, CONV_HIST, D_MODEL), lambda s: (cur(s) // nl, 0, 0)),
            _const_spec((1, D_MODEL)),
            _const_spec((D_MODEL, 2 * D_MODEL)),
            _const_spec((1, 2 * D_MODEL)),
            _const_spec((CONV_WIDTH, SUBLANES, D_MODEL)),
            _const_spec((1, D_MODEL)),
            _const_spec((1, D_MODEL)),
            _const_spec((1, D_MODEL)),
            _const_spec((D_MODEL, D_MODEL)),
            _const_spec((1, D_MODEL)),
            _const_spec((1, D_MODEL)),
            _const_spec((D_MODEL, D_FF)),
            _const_spec((D_MODEL, D_FF)),
            _const_spec((D_FF, D_MODEL)),
        ],
        out_specs=[
            pl.BlockSpec((1, tl, D_MODEL), lambda s: (prev(s) // nl, prev(s) % nl, 0)),
            pl.BlockSpec((1, CONV_HIST, D_MODEL), lambda s: (cur(s) // nl, 0, 0)),
        ],
        out_shape=[
            jax.ShapeDtypeStruct((bsz, seq, D_MODEL), F32),
            jax.ShapeDtypeStruct((bsz, CONV_HIST, D_MODEL), F32),
        ],
        scratch_shapes=[
            pltpu.VMEM((1, HIST_PAD + tl, D_MODEL), F32),
            pltpu.VMEM((1, tl, D_MODEL), F32),
            pltpu.VMEM((2, tl, D_MODEL), F32),
        ],
        compiler_params=_params(1),
        name="layer0",
    )(h, hist, w["gm0"], w["w_pw1"], w["b_pw1"], w["w_dw"], w["b_dw"], w["ln_g"], w["ln_b"], w["w_pw2"], w["b_pw2"],
      w["gf0"], w["wg0"], w["wu0"], w["wd0"])


def _softplus(x):
    return jnp.maximum(x, 0.0) + jnp.log(1.0 + jnp.exp(-jnp.abs(x)))


def _ssd_front_kernel(h_ref, ch_ref, gm_ref, wz_ref, wxa_ref, wxb_ref, wdt_ref, cw_ref, cb_ref, dtb_ref,
                      z_ref, xbc_ref, dt_ref, nc_ref, hist_ref, *, tb, tl):
    l = pl.program_id(1)
    m = tb * tl

    @pl.when(l == 0)
    def _():
        hist_ref[:, 0:SSD_HIST_PAD - SSD_CONV_HIST, :] = jnp.zeros((tb, SSD_HIST_PAD - SSD_CONV_HIST, CONV_DIM), F32)
        hist_ref[:, SSD_HIST_PAD - SSD_CONV_HIST:, :] = ch_ref[...]

    x = h_ref[...].reshape(m, D_MODEL)
    hn = _rms(x, gm_ref[...]).astype(BF16)
    dt_ref[...] = _softplus(_dot(hn, wdt_ref[...]) + dtb_ref[...]).reshape(tb, tl, LANES)
    row_in_group = lax.broadcasted_iota(jnp.int32, (tb, tl, FRONT_COLS), 1) % SUBLANES

    def conv_block(xn, lo):
        cs = slice(lo, lo + FRONT_COLS)
        xh = jnp.concatenate([hist_ref[:, :, cs], xn], axis=1)
        tap = lambda k: jnp.tile(cw_ref[k, :, cs], (tl // SUBLANES, 1))[None]
        acc = cb_ref[:, cs][None] + xn * tap(SSD_CONV_WIDTH - 1)
        for d in range(1, SSD_CONV_WIDTH):
            rot = _roll_groups_down(xh, d)
            acc = acc + jnp.where(row_in_group < d, rot[:, :tl], rot[:, SUBLANES:]) * tap(SSD_CONV_WIDTH - 1 - d)
        xbc_ref[:, :, cs] = _silu(acc).astype(xbc_ref.dtype)
        hist_ref[:, :, cs] = xn[:, tl - SSD_HIST_PAD:]

    def z_block(zn, lo):
        z_ref[:, :, lo:lo + FRONT_COLS] = zn.astype(z_ref.dtype)

    tasks = []
    z_cols = list(range(0, D_INNER, FRONT_COLS))
    for i, lo in enumerate(range(0, CONV_DIM, FRONT_COLS)):
        w_ref, off = (wxa_ref, lo) if lo < D_INNER else (wxb_ref, lo - D_INNER)
        tasks.append((w_ref, off, conv_block, lo))
        if i % 2 == 1:
            tasks.append((wz_ref, z_cols[i // 2], z_block, z_cols[i // 2]))
    pending = None
    for w_ref, off, finish, lo in tasks:
        res = _dot(hn, w_ref[:, off:off + FRONT_COLS]).reshape(tb, tl, FRONT_COLS)
        if pending is not None:
            pending[0](pending[1], pending[2])
        pending = (finish, res, lo)
    pending[0](pending[1], pending[2])

    @pl.when(l == pl.num_programs(1) - 1)
    def _():
        nc_ref[...] = hist_ref[:, SSD_HIST_PAD - SSD_CONV_HIST:, :]


def _ssd_front(h, ch, gm, w_in, wdt, cw, cb, dtb, *, tb, tl):
    bsz, seq, _ = h.shape
    act_dtype = BF16 if tl % (2 * SUBLANES) == 0 else F32
    w_blk = lambda j: pl.BlockSpec((D_MODEL, D_INNER), lambda *_: (0, j), pipeline_mode=pl.Buffered(1))
    return pl.pallas_call(
        functools.partial(_ssd_front_kernel, tb=tb, tl=tl),
        grid=(bsz // tb, seq // tl),
        in_specs=[
            pl.BlockSpec((tb, tl, D_MODEL), lambda b, l: (b, l, 0)),
            pl.BlockSpec((tb, SSD_CONV_HIST, CONV_DIM), lambda b, l: (b, 0, 0)),
            _const_spec((1, D_MODEL)),
            w_blk(0),
            w_blk(1),
            w_blk(2),
            _const_spec((D_MODEL, LANES)),
            _const_spec((SSD_CONV_WIDTH, SUBLANES, CONV_DIM)),
            _const_spec((1, CONV_DIM)),
            _const_spec((1, LANES)),
        ],
        out_specs=[
            pl.BlockSpec((tb, tl, D_INNER), lambda b, l: (b, l, 0)),
            pl.BlockSpec((tb, tl, CONV_DIM), lambda b, l: (b, l, 0)),
            pl.BlockSpec((tb, tl, LANES), lambda b, l: (b, l, 0)),
            pl.BlockSpec((tb, SSD_CONV_HIST, CONV_DIM), lambda b, l: (b, 0, 0)),
        ],
        out_shape=[
            jax.ShapeDtypeStruct((bsz, seq, D_INNER), act_dtype),
            jax.ShapeDtypeStruct((bsz, seq, CONV_DIM), act_dtype),
            jax.ShapeDtypeStruct((bsz, seq, LANES), F32),
            jax.ShapeDtypeStruct((bsz, SSD_CONV_HIST, CONV_DIM), F32),
        ],
        scratch_shapes=[pltpu.VMEM((tb, SSD_HIST_PAD, CONV_DIM), F32)],
        compiler_params=_params(2),
        name="ssd_front",
    )(h, ch, gm, w_in, w_in, w_in, wdt, cw, cb, dtb)


def _cumsum_rows(a, tri):
    a_hi = a.astype(BF16)
    r1 = a - a_hi.astype(F32)
    a_mid = r1.astype(BF16)
    a_lo = (r1 - a_mid.astype(F32)).astype(BF16)
    return _dot(tri, a_hi) + _dot(tri, a_mid) + _dot(tri, a_lo)


def _head_rows(mat, g, width):
    return jnp.concatenate(
        [jnp.broadcast_to(mat[HEADS_PER_GROUP * g + r:HEADS_PER_GROUP * g + r + 1, :], (HEAD_DIM, width))
         for r in range(HEADS_PER_GROUP)], axis=0)


def _ssd_scan_kernel(xbc_ref, dt_ref, dtn_ref, z_ref, h_ref, h0_ref, alog_ref, dske_ref, ng_ref, wo_ref, o_ref, st_ref,
                     vn_ref, dec_ref, *, lc, nb):
    c = pl.program_id(1)
    slot = c % 2

    row = lax.broadcasted_iota(jnp.int32, (lc, lc), 0)
    col = lax.broadcasted_iota(jnp.int32, (lc, lc), 1)
    causal = row >= col
    tri = jnp.where(causal, 1.0, 0.0).astype(BF16)
    lane_head = lax.broadcasted_iota(jnp.int32, (lc, GROUP_DIM), 1) // HEAD_DIM
    lane_lo = lax.broadcasted_iota(jnp.int32, (lc, LANES), 1) < HEAD_DIM
    neg_a = -jnp.exp(alog_ref[...])

    def group_dots(i, g):
        b_off = D_INNER + g * D_STATE
        c_off = D_INNER + N_GROUPS * D_STATE + g * D_STATE
        bg = xbc_ref[i, :, b_off:b_off + D_STATE].astype(BF16)
        cg = xbc_ref[i, :, c_off:c_off + D_STATE].astype(BF16)
        cb = lax.dot_general(cg, bg, _NT, preferred_element_type=F32)
        hg = st_ref[i, g]
        yoff = lax.dot_general(cg, hg.astype(BF16), _NT, preferred_element_type=F32)
        return bg, cb, hg, yoff

    def decays(ref, i):
        dt = ref[i]
        acs = _cumsum_rows(dt * neg_a, tri)
        acs_t = acs.T
        dt_t = dt.T
        last_t = jnp.broadcast_to(acs_t[:, lc - 1:lc], (LANES, lc))
        c_t = acs_t - jnp.log(dt_t)
        coef_t = jnp.exp(last_t - acs_t) * dt_t
        cd = jnp.broadcast_to(jnp.exp(acs_t[:, lc - 1:lc]), (LANES, D_STATE))
        return acs, c_t, coef_t, cd

    def group_tail(i, g, dots, dec):
        bg, cb, hg, yoff = dots
        acs, c_t, coef_t, cd = dec
        lo, hi = g * GROUP_DIM, (g + 1) * GROUP_DIM
        xg = xbc_ref[i, :, lo:hi].astype(F32)
        ws, acs_b = [], []
        for r in range(HEADS_PER_GROUP):
            h = HEADS_PER_GROUP * g + r
            ab = jnp.broadcast_to(acs[:, h:h + 1], (lc, LANES))
            acs_b.append(ab)
            ws.append((cb * jnp.exp(jnp.where(causal, ab - c_t[h:h + 1, :], -jnp.inf))).astype(BF16))
        x_bd = jnp.concatenate([jnp.where(lane_head == r, xg, 0.0).astype(BF16) for r in range(HEADS_PER_GROUP)],
                               axis=0)
        yd = _dot(jnp.concatenate(ws, axis=1), x_bd)
        e_b = jnp.exp(jnp.concatenate([jnp.where(lane_lo, acs_b[0], acs_b[1]),
                                       jnp.where(lane_lo, acs_b[2], acs_b[3])], axis=1))
        y = yd + e_b * yoff + dske_ref[:, lo:hi] * xg

        xcoef_t = (xg.T * _head_rows(coef_t, g, lc)).astype(BF16)
        st_ref[i, g] = _head_rows(cd, g, D_STATE) * hg + _dot(xcoef_t, bg)

        v = y * _silu(z_ref[i, :, lo:hi].astype(F32))
        vn = v * lax.rsqrt(jnp.mean(v * v, axis=-1, keepdims=True) + EPS) * ng_ref[:, lo:hi]
        vn_ref[i, :, lo:hi] = vn.astype(BF16)

    @pl.when(c == 0)
    def _():
        st_ref[...] = h0_ref[...]
        for i in range(nb):
            for k, v in enumerate(decays(dt_ref, i)):
                dec_ref[0, i, k] = v

    half = D_INNER // 2
    dots = [group_dots(i, 0) for i in range(nb)]
    dec = [tuple(dec_ref[slot, i, k] for k in range(4)) for i in range(nb)]
    for g in range(N_GROUPS):
        nxt = [group_dots(i, g + 1) for i in range(nb)] if g + 1 < N_GROUPS else None
        for i in range(nb):
            group_tail(i, g, dots[i], dec[i])
        dots = nxt
        if g == 1:
            for i in range(nb):
                for k, v in enumerate(decays(dtn_ref, i)):
                    dec_ref[1 - slot, i, k] = v
        if g == N_GROUPS // 2 - 1:
            for i in range(nb):
                o_ref[i] = h_ref[i] + _dot(vn_ref[i, :, :half], wo_ref[:half, :])
    for i in range(nb):
        o_ref[i] = o_ref[i] + _dot(vn_ref[i, :, half:], wo_ref[half:, :])


def _ssd_scan(xbc, dt, z, h, h0, alog, dske, ng, wo, *, lc):
    bsz, seq, _ = xbc.shape
    assert lc == LANES
    nb = 1
    last = seq // lc - 1
    return pl.pallas_call(
        functools.partial(_ssd_scan_kernel, lc=lc, nb=nb),
        grid=(bsz // nb, seq // lc),
        in_specs=[
            pl.BlockSpec((nb, lc, CONV_DIM), lambda b, c: (b, c, 0)),
            pl.BlockSpec((nb, lc, LANES), lambda b, c: (b, c, 0)),
            pl.BlockSpec((nb, lc, LANES), lambda b, c: (b, jnp.minimum(c + 1, last), 0)),
            pl.BlockSpec((nb, lc, D_INNER), lambda b, c: (b, c, 0)),
            pl.BlockSpec((nb, lc, D_MODEL), lambda b, c: (b, c, 0)),
            pl.BlockSpec((nb, N_GROUPS, GROUP_DIM, D_STATE), lambda b, c: (b, 0, 0, 0)),
            _const_spec((1, LANES)),
            _const_spec((1, D_INNER)),
            _const_spec((1, D_INNER)),
            _const_spec((D_INNER, D_MODEL)),
        ],
        out_specs=[
            pl.BlockSpec((nb, lc, D_MODEL), lambda b, c: (b, c, 0)),
            pl.BlockSpec((nb, N_GROUPS, GROUP_DIM, D_STATE), lambda b, c: (b, 0, 0, 0)),
        ],
        out_shape=[
            jax.ShapeDtypeStruct((bsz, seq, D_MODEL), F32),
            jax.ShapeDtypeStruct((bsz, N_GROUPS, GROUP_DIM, D_STATE), F32),
        ],
        scratch_shapes=[pltpu.VMEM((nb, lc, D_INNER), BF16), pltpu.VMEM((2, nb, 4, LANES, LANES), F32)],
        compiler_params=_params(2),
        name="ssd_scan",
    )(xbc, dt, dt, z, h, h0, alog, dske, ng, wo)


def _ssd_step_kernel(x_ref, b_ref, c_ref, dt_ref, st_ref, alog_ref, dsk_ref, y_ref, so_ref, *, nb, ls):
    g = pl.program_id(1)
    rows = nb * ls
    shift = (LANES - HEADS_PER_GROUP * g) % LANES
    dt = pltpu.roll(dt_ref[...], shift, axis=1)
    alog = pltpu.roll(alog_ref[...], shift, axis=1)
    dsk = pltpu.roll(dsk_ref[...], shift, axis=1)
    a = dt * (-jnp.exp(alog))

    row = lax.broadcasted_iota(jnp.int32, (rows, rows), 0)
    col = lax.broadcasted_iota(jnp.int32, (rows, rows), 1)
    same = (row // ls) == (col // ls)
    causal = jnp.logical_and(same, row >= col)
    hi = lax.Precision.HIGHEST
    acs = jnp.dot(jnp.where(causal, 1.0, 0.0), a, preferred_element_type=F32, precision=hi)
    tot = jnp.dot(jnp.where(same, 1.0, 0.0), a, preferred_element_type=F32, precision=hi)
    acs_t = acs.T
    coef = jnp.exp(tot - acs) * dt
    eacs = jnp.exp(acs)
    etot = jnp.exp(tot)

    bg = b_ref[...].astype(BF16)
    cg = c_ref[...].astype(BF16)
    xg = x_ref[...]
    cb = lax.dot_general(cg, bg, _NT, preferred_element_type=F32)

    seq_of_col = lax.broadcasted_iota(jnp.int32, (GROUP_DIM, rows), 1) // ls
    yoff_t = jnp.zeros((GROUP_DIM, rows), F32)
    for b in range(nb):
        yb = lax.dot_general(st_ref[b, 0].astype(BF16), cg, _NT, preferred_element_type=F32)
        yoff_t = jnp.where(seq_of_col == b, yb, yoff_t)
    yoff = yoff_t.T

    ys, xcs = [], []
    for r in range(HEADS_PER_GROUP):
        decay = jnp.exp(jnp.where(causal, acs[:, r:r + 1] - acs_t[r:r + 1, :], -jnp.inf))
        w = (cb * decay).astype(BF16)
        xr = xg[:, r * HEAD_DIM:(r + 1) * HEAD_DIM]
        yd = _dot(w, (xr * dt[:, r:r + 1]).astype(BF16))
        ys.append(yd + eacs[:, r:r + 1] * yoff[:, r * HEAD_DIM:(r + 1) * HEAD_DIM] + dsk[:, r:r + 1] * xr)
        xcs.append(xr * coef[:, r:r + 1])
    y_ref[...] = jnp.concatenate(ys, axis=1)
    xcoef_t = jnp.concatenate(xcs, axis=1).T

    for b in range(nb):
        upd = _dot(jnp.where(seq_of_col == b, xcoef_t, 0.0).astype(BF16), bg)
        scale = jnp.concatenate(
            [jnp.broadcast_to(etot[b * ls:b * ls + 1, r:r + 1], (HEAD_DIM, D_STATE)) for r in range(HEADS_PER_GROUP)],
            axis=0)
        so_ref[b, 0] = scale * st_ref[b, 0] + upd


def _ssd_step(xbc, dt, st, alog, dsk, *, nb, ls):
    t = xbc.shape[0]
    n_seq = st.shape[0]
    rows = nb * ls
    b_blk0 = D_INNER // D_STATE
    c_blk0 = b_blk0 + N_GROUPS
    return pl.pallas_call(
        functools.partial(_ssd_step_kernel, nb=nb, ls=ls),
        grid=(n_seq // nb, N_GROUPS),
        in_specs=[
            pl.BlockSpec((rows, GROUP_DIM), lambda i, g: (i, g)),
            pl.BlockSpec((rows, D_STATE), lambda i, g: (i, b_blk0 + g)),
            pl.BlockSpec((rows, D_STATE), lambda i, g: (i, c_blk0 + g)),
            pl.BlockSpec((rows, LANES), lambda i, g: (i, 0)),
            pl.BlockSpec((nb, 1, GROUP_DIM, D_STATE), lambda i, g: (i, g, 0, 0)),
            _const_spec((1, LANES)),
            _const_spec((1, LANES)),
        ],
        out_specs=[
            pl.BlockSpec((rows, GROUP_DIM), lambda i, g: (i, g)),
            pl.BlockSpec((nb, 1, GROUP_DIM, D_STATE), lambda i, g: (i, g, 0, 0)),
        ],
        out_shape=[
            jax.ShapeDtypeStruct((t, D_INNER), F32),
            jax.ShapeDtypeStruct((n_seq, N_GROUPS, GROUP_DIM, D_STATE), F32),
        ],
        compiler_params=_params(2),
        name="ssd_step",
    )(xbc, xbc, xbc, dt, st, alog, dsk)


def _ssd_post_kernel(y_ref, z_ref, h_ref, ng_ref, wo_ref, o_ref):
    acc = h_ref[...]
    for g in range(N_GROUPS):
        lo, hi = g * GROUP_DIM, (g + 1) * GROUP_DIM
        v = y_ref[:, lo:hi] * _silu(z_ref[:, lo:hi])
        vn = v * lax.rsqrt(jnp.mean(v * v, axis=-1, keepdims=True) + EPS) * ng_ref[:, lo:hi]
        acc = acc + _dot(vn.astype(BF16), wo_ref[lo:hi, :])
    o_ref[...] = acc


def _ssd_post(y, z, h, ng, wo, *, tm):
    t = y.shape[0]
    return pl.pallas_call(
        _ssd_post_kernel,
        grid=(t // tm,),
        in_specs=[
            pl.BlockSpec((tm, D_INNER), lambda i: (i, 0)),
            pl.BlockSpec((tm, D_INNER), lambda i: (i, 0)),
            pl.BlockSpec((tm, D_MODEL), lambda i: (i, 0)),
            _const_spec((1, D_INNER)),
            _const_spec((D_INNER, D_MODEL)),
        ],
        out_specs=pl.BlockSpec((tm, D_MODEL), lambda i: (i, 0)),
        out_shape=jax.ShapeDtypeStruct((t, D_MODEL), F32),
        compiler_params=_params(1),
        name="ssd_post",
    )(y, z, h, ng, wo)


def _trunk(h, cc_hist, sc_hist, ssm, w, *, tb, tl, tm, short):
    bsz, seq, _ = h.shape
    t = bsz * seq
    if tb == 1 and tl % 64 == 0:
        h2, new_cc = _layer0(h, cc_hist, w, tl=tl)
        h2 = h2.reshape(t, D_MODEL)
    else:
        h1, new_cc = _conformer(h, cc_hist, w["gm0"], w["w_pw1"], w["b_pw1"], w["w_dw"], w["b_dw"], w["ln_g"],
                                w["ln_b"], w["w_pw2"], w["b_pw2"], tb=tb, tl=tl)
        h2 = _ffn(h1.reshape(t, D_MODEL), w["gf0"], w["wg0"], w["wu0"], w["wd0"], w["g_final"], tm=tm, final=False)
    z, xbc, dt, new_sc = _ssd_front(h2.reshape(bsz, seq, D_MODEL), sc_hist, w["gm1"], w["w_in"], w["w_dt"],
                                    w["conv_w"], w["conv_b"], w["dt_bias"], tb=tb, tl=tl)
    if short:
        y, new_ss = _ssd_step(xbc.reshape(t, CONV_DIM), dt.reshape(t, LANES), ssm, w["a_log"], w["d_skip"],
                              nb=LANES // seq, ls=seq)
        h3 = _ssd_post(y, z.reshape(t, D_INNER), h2, w["norm_g"], w["w_out"], tm=tm)
    else:
        pad = (-seq) % LANES
        rows = lambda v: jnp.pad(v, ((0, 0), (0, pad), (0, 0))) if pad else v
        h3, new_ss = _ssd_scan(rows(xbc), rows(dt), rows(z), rows(h2.reshape(bsz, seq, D_MODEL)), ssm,
                               w["a_log"], w["d_skip_lanes"], w["norm_g"], w["w_out"], lc=LANES)
        h3 = h3[:, :seq].reshape(t, D_MODEL)
    out = _ffn(h3, w["gf1"], w["wg1"], w["wu1"], w["wd1"], w["g_final"], tm=tm, final=True)
    return out.reshape(bsz, seq, D_MODEL), new_cc, new_sc, new_ss


def _pad_lanes(v):
    return jnp.pad(v.astype(F32), (0, LANES - v.shape[0])).reshape(1, LANES)


def kernel(x_prompt, x_sample, cache_conv, state_ssd_conv, state_ssm, meta_tokens, norm_mix, norm_ffn, norm_final, cf_w_pw1, cf_b_pw1, cf_w_dw, cf_b_dw, cf_ln_g, cf_ln_b, cf_w_pw2, cf_b_pw2, ssd_w_in, ssd_conv_w, ssd_conv_b, ssd_dt_bias, ssd_a_log, ssd_d, ssd_norm_g, ssd_w_out, ffn_w_gate, ffn_w_up, ffn_w_down):
    row = lambda v: v.astype(F32).reshape(1, -1)
    w_in = ssd_w_in[0]
    w = {
        "gm0": row(norm_mix[0]), "gm1": row(norm_mix[1]),
        "gf0": row(norm_ffn[0]), "gf1": row(norm_ffn[1]), "g_final": row(norm_final),
        "w_pw1": cf_w_pw1[0].astype(BF16), "b_pw1": row(cf_b_pw1[0]),
        "w_dw": jnp.broadcast_to(cf_w_dw[0][:, None, :], (CONV_WIDTH, SUBLANES, D_MODEL)),
        "b_dw": row(cf_b_dw[0]), "ln_g": row(cf_ln_g[0]), "ln_b": row(cf_ln_b[0]),
        "w_pw2": cf_w_pw2[0].astype(BF16), "b_pw2": row(cf_b_pw2[0]),
        "w_in": w_in.astype(BF16),
        "w_dt": jnp.pad(w_in[:, D_INNER + CONV_DIM:], ((0, 0), (0, LANES - N_HEADS))).astype(BF16),
        "conv_w": jnp.broadcast_to(ssd_conv_w[0][:, None, :], (SSD_CONV_WIDTH, SUBLANES, CONV_DIM)),
        "conv_b": row(ssd_conv_b[0]),
        "dt_bias": _pad_lanes(ssd_dt_bias[0]), "a_log": _pad_lanes(ssd_a_log[0]), "d_skip": _pad_lanes(ssd_d[0]),
        "d_skip_lanes": jnp.repeat(ssd_d[0].astype(F32), HEAD_DIM).reshape(1, D_INNER),
        "norm_g": row(ssd_norm_g[0]), "w_out": ssd_w_out[0].astype(BF16),
        "wg0": ffn_w_gate[0].astype(BF16), "wu0": ffn_w_up[0].astype(BF16), "wd0": ffn_w_down[0].astype(BF16),
        "wg1": ffn_w_gate[1].astype(BF16), "wu1": ffn_w_up[1].astype(BF16), "wd1": ffn_w_down[1].astype(BF16),
    }
    bp, seq, _ = x_prompt.shape
    bs, dseq, _ = x_sample.shape

    _, cc_m, sc_m, ss_m = _trunk(
        meta_tokens.astype(F32)[None], jnp.zeros((1, CONV_HIST, D_MODEL), F32),
        jnp.zeros((1, SSD_CONV_HIST, CONV_DIM), F32), jnp.zeros((1, N_GROUPS, GROUP_DIM, D_STATE), F32), w,
        tb=1, tl=N_META, tm=N_META, short=False)

    y_prompt, cc_p, sc_p, ss_p = _trunk(
        x_prompt, jnp.broadcast_to(cc_m, (bp, CONV_HIST, D_MODEL)),
        jnp.broadcast_to(sc_m, (bp, SSD_CONV_HIST, CONV_DIM)),
        jnp.broadcast_to(ss_m, (bp, N_GROUPS, GROUP_DIM, D_STATE)), w,
        tb=1, tl=256, tm=512, short=False)

    y_sample, cc_s, sc_s, ss_s = _trunk(
        x_sample, cache_conv[0], state_ssd_conv[0], state_ssm[0].reshape(bs, N_GROUPS, GROUP_DIM, D_STATE), w,
        tb=32, tl=dseq, tm=512, short=True)

    unpack_ss = lambda v: v.reshape(1, v.shape[0], N_HEADS, HEAD_DIM, D_STATE)
    return (y_prompt, y_sample, cc_p[None], cc_s[None], sc_p[None], sc_s[None], unpack_ss(ss_p), unpack_ss(ss_s))
```

```python
import functools

import jax
import jax.numpy as jnp
from jax import lax
from jax.experimental import pallas as pl
from jax.experimental.pallas import tpu as pltpu

D_MODEL = 1024
N_META = 16
CONV_WIDTH = 31
CONV_HIST = CONV_WIDTH - 1
D_INNER = 2048
HEAD_DIM = 64
N_HEADS = 32
N_GROUPS = 8
HEADS_PER_GROUP = 4
GROUP_DIM = HEADS_PER_GROUP * HEAD_DIM
D_STATE = 128
SSD_CONV_WIDTH = 4
SSD_CONV_HIST = SSD_CONV_WIDTH - 1
CONV_DIM = D_INNER + 2 * N_GROUPS * D_STATE
D_FF = 2816
EPS = 1e-6

LANES = 128
SUBLANES = 8
HIST_PAD = 32
SSD_HIST_PAD = 8
FRONT_COLS = 256
FFN_COLS = 512
VMEM_LIMIT = 56 * 1024 * 1024

F32 = jnp.float32
BF16 = jnp.bfloat16
_NT = (((1,), (1,)), ((), ()))


def _const_spec(shape):
    return pl.BlockSpec(shape, lambda *_: (0,) * len(shape), pipeline_mode=pl.Buffered(1))


def _params(n_axes):
    return pltpu.CompilerParams(dimension_semantics=("arbitrary",) * n_axes, vmem_limit_bytes=VMEM_LIMIT)


def _rms(x, g):
    return x * lax.rsqrt(jnp.mean(x * x, axis=-1, keepdims=True) + EPS) * g


def _sigmoid(x):
    return 1.0 / (1.0 + jnp.exp(-x))


def _silu(x):
    return x * _sigmoid(x)


def _dot(a, b):
    return jnp.dot(a, b, preferred_element_type=F32)


def _roll_groups_down(x, d):
    b, n, c = x.shape
    return pltpu.roll(x.reshape(b * n // SUBLANES, SUBLANES, c), d, axis=1).reshape(b, n, c)


def _shift_rows_up(x, s):
    n = x.shape[0] - SUBLANES
    rot = [pltpu.roll(x[j:j + SUBLANES], SUBLANES - s, axis=0) for j in range(0, n + SUBLANES, SUBLANES)]
    keep = lax.broadcasted_iota(jnp.int32, (SUBLANES, x.shape[1]), 0) < SUBLANES - s
    return jnp.concatenate([jnp.where(keep, rot[j], rot[j + 1]) for j in range(n // SUBLANES)], axis=0)


def _dwconv(src_ref, w_ref, bias_ref, dst_ref, *, tb, tl, rc, cc, taps, base, act, cols=None, static=False):
    n_rc = tl // rc
    c_lo, c_hi = cols if cols is not None else (0, src_ref.shape[-1])
    by_shift = {}
    for k in range(taps):
        by_shift.setdefault((base + k) % SUBLANES, []).append(k)

    def piece(b, r0, c0):
        acc = jnp.broadcast_to(bias_ref[:, c0:c0 + cc], (rc, cc))
        for s, ks in sorted(by_shift.items()):
            n_rows = rc + SUBLANES if s else rc
            part = None
            for k in ks:
                q = (base + k) // SUBLANES
                blk = src_ref[b, pl.ds(r0 + SUBLANES * q, n_rows), c0:c0 + cc]
                wk = jnp.tile(w_ref[k, :, c0:c0 + cc], (n_rows // SUBLANES, 1))
                part = blk * wk if part is None else part + blk * wk
            acc = acc + (_shift_rows_up(part, s) if s else part)
        dst_ref[b, pl.ds(r0, rc), c0:c0 + cc] = act(acc)

    def chunk(b, r0):
        for c0 in range(c_lo, c_hi, cc):
            piece(b, r0, c0)

    if static == "deferred":
        return [functools.partial(piece, b, j * rc, c0)
                for b in range(tb) for j in range(n_rc) for c0 in range(c_lo, c_hi, cc)]
    if static:
        for b in range(tb):
            for j in range(n_rc):
                chunk(b, j * rc)
    else:
        def body(i, carry):
            chunk(i // n_rc, pl.multiple_of((i % n_rc) * rc, SUBLANES))
            return carry

        lax.fori_loop(0, tb * n_rc, body, 0)


def _conformer_kernel(h_ref, hist_ref, gm_ref, w1_ref, b1_ref, wdw_ref, bdw_ref, lng_ref, lnb_ref, w2_ref, b2_ref,
                      out_ref, nh_ref, gh_ref, c_ref, *, tb, tl, rc):
    l = pl.program_id(1)
    m = tb * tl

    @pl.when(l == 0)
    def _():
        gh_ref[:, 0:HIST_PAD - CONV_HIST, :] = jnp.zeros((tb, HIST_PAD - CONV_HIST, D_MODEL), F32)
        gh_ref[:, HIST_PAD - CONV_HIST:HIST_PAD, :] = hist_ref[...]

    x = h_ref[...].reshape(m, D_MODEL)
    hn = _rms(x, gm_ref[...]).astype(BF16)
    u = _dot(hn, w1_ref[...]) + b1_ref[...]
    g = u[:, :D_MODEL] * _sigmoid(u[:, D_MODEL:])
    gh_ref[:, HIST_PAD:HIST_PAD + tl, :] = g.reshape(tb, tl, D_MODEL)

    _dwconv(gh_ref, wdw_ref, bdw_ref, c_ref, tb=tb, tl=tl, rc=rc, cc=LANES if rc >= 64 else 4 * LANES,
            taps=CONV_WIDTH, base=HIST_PAD - CONV_HIST, act=lambda v: v)

    @pl.when(l == pl.num_programs(1) - 1)
    def _():
        nh_ref[...] = gh_ref[:, tl + HIST_PAD - CONV_HIST:tl + HIST_PAD, :]

    gh_ref[:, 0:HIST_PAD, :] = gh_ref[:, tl:tl + HIST_PAD, :]

    c = c_ref[...].reshape(m, D_MODEL)
    mu = jnp.mean(c, axis=-1, keepdims=True)
    xc = c - mu
    cn = xc * lax.rsqrt(jnp.mean(xc * xc, axis=-1, keepdims=True) + EPS) * lng_ref[...] + lnb_ref[...]
    cn = _silu(cn).astype(BF16)
    y = _dot(cn, w2_ref[...]) + b2_ref[...] + x
    out_ref[...] = y.reshape(tb, tl, D_MODEL)


def _conformer(h, hist, gm, w1, b1, wdw, bdw, lng, lnb, w2, b2, *, tb, tl):
    bsz, seq, _ = h.shape
    rc = 64 if tl % 64 == 0 else (16 if tl % 16 == 0 else SUBLANES)
    kern = functools.partial(_conformer_kernel, tb=tb, tl=tl, rc=rc)
    return pl.pallas_call(
        kern,
        grid=(bsz // tb, seq // tl),
        in_specs=[
            pl.BlockSpec((tb, tl, D_MODEL), lambda b, l: (b, l, 0)),
            pl.BlockSpec((tb, CONV_HIST, D_MODEL), lambda b, l: (b, 0, 0)),
            _const_spec((1, D_MODEL)),
            _const_spec((D_MODEL, 2 * D_MODEL)),
            _const_spec((1, 2 * D_MODEL)),
            _const_spec((CONV_WIDTH, SUBLANES, D_MODEL)),
            _const_spec((1, D_MODEL)),
            _const_spec((1, D_MODEL)),
            _const_spec((1, D_MODEL)),
            _const_spec((D_MODEL, D_MODEL)),
            _const_spec((1, D_MODEL)),
        ],
        out_specs=[
            pl.BlockSpec((tb, tl, D_MODEL), lambda b, l: (b, l, 0)),
            pl.BlockSpec((tb, CONV_HIST, D_MODEL), lambda b, l: (b, 0, 0)),
        ],
        out_shape=[
            jax.ShapeDtypeStruct((bsz, seq, D_MODEL), F32),
            jax.ShapeDtypeStruct((bsz, CONV_HIST, D_MODEL), F32),
        ],
        scratch_shapes=[
            pltpu.VMEM((tb, HIST_PAD + tl, D_MODEL), F32),
            pltpu.VMEM((tb, tl, D_MODEL), F32),
        ],
        compiler_params=_params(2),
        name="conformer",
    )(h, hist, gm, w1, b1, wdw, bdw, lng, lnb, w2, b2)


def _ffn_kernel(x_ref, g_ref, wg_ref, wu_ref, wd_ref, gf_ref, o_ref, *, final):
    x = x_ref[...]
    hn = _rms(x, g_ref[...]).astype(BF16)
    a = _dot(hn, wg_ref[...])
    b = _dot(hn, wu_ref[...])
    t = (_silu(a) * b).astype(BF16)
    y = x + _dot(t, wd_ref[...])
    if final:
        y = _rms(y, gf_ref[...])
    o_ref[...] = y


def _ffn(x, g, wg, wu, wd, gf, *, tm, final):
    t = x.shape[0]
    return pl.pallas_call(
        functools.partial(_ffn_kernel, final=final),
        grid=(t // tm,),
        in_specs=[
            pl.BlockSpec((tm, D_MODEL), lambda i: (i, 0)),
            _const_spec((1, D_MODEL)),
            _const_spec((D_MODEL, D_FF)),
            _const_spec((D_MODEL, D_FF)),
            _const_spec((D_FF, D_MODEL)),
            _const_spec((1, D_MODEL)),
        ],
        out_specs=pl.BlockSpec((tm, D_MODEL), lambda i: (i, 0)),
        out_shape=jax.ShapeDtypeStruct((t, D_MODEL), F32),
        compiler_params=_params(1),
        name="ffn_final" if final else "ffn",
    )(x, g, wg, wu, wd, gf)


def _layer0_kernel(h_ref, hist_ref, gm_ref, w1_ref, b1_ref, wdw_ref, bdw_ref, lng_ref, lnb_ref, w2_ref, b2_ref,
                   gf_ref, wg_ref, wu_ref, wd_ref, out_ref, nh_ref, gh_ref, c_ref, h1_ref, *, tl, rc, nl, nt):
    s = pl.program_id(0)
    slot = s % 2
    tile = jnp.minimum(s, nt - 1)

    @pl.when(s == 0)
    def _():
        h1_ref[1] = jnp.zeros((tl, D_MODEL), F32)

    @pl.when(tile % nl == 0)
    def _():
        gh_ref[:, 0:HIST_PAD - CONV_HIST, :] = jnp.zeros((1, HIST_PAD - CONV_HIST, D_MODEL), F32)
        gh_ref[:, HIST_PAD - CONV_HIST:HIST_PAD, :] = hist_ref[...]

    x1 = h1_ref[1 - slot]
    hn1 = _rms(x1, gf_ref[...]).astype(BF16)
    out_ref[0] = x1

    def ffn_block(lo, hi):
        t = (_silu(_dot(hn1, wg_ref[:, lo:hi])) * _dot(hn1, wu_ref[:, lo:hi])).astype(BF16)
        out_ref[0] = out_ref[0] + _dot(t, wd_ref[lo:hi, :])

    x = h_ref[0]
    hn = _rms(x, gm_ref[...]).astype(BF16)
    u = _dot(hn, w1_ref[...]) + b1_ref[...]
    gh_ref[0, HIST_PAD:HIST_PAD + tl, :] = u[:, :D_MODEL] * _sigmoid(u[:, D_MODEL:])
    pieces = _dwconv(gh_ref, wdw_ref, bdw_ref, c_ref, tb=1, tl=tl, rc=rc, cc=LANES, taps=CONV_WIDTH,
                     base=HIST_PAD - CONV_HIST, act=lambda v: v, static="deferred")
    blocks = [(lo, min(lo + FFN_COLS, D_FF)) for lo in range(0, D_FF, FFN_COLS)]
    per_block = -(-len(pieces) // len(blocks))
    @pl.when(s >= 0)
    def _():
        for j, (lo, hi) in enumerate(blocks):
            ffn_block(lo, hi)
            for p in pieces[j * per_block:(j + 1) * per_block]:
                p()
    nh_ref[...] = gh_ref[:, tl + HIST_PAD - CONV_HIST:tl + HIST_PAD, :]
    gh_ref[:, 0:HIST_PAD, :] = gh_ref[:, tl:tl + HIST_PAD, :]
    c = c_ref[0]
    xc = c - jnp.mean(c, axis=-1, keepdims=True)
    cn = xc * lax.rsqrt(jnp.mean(xc * xc, axis=-1, keepdims=True) + EPS) * lng_ref[...] + lnb_ref[...]
    h1_ref[slot] = _dot(_silu(cn).astype(BF16), w2_ref[...]) + b2_ref[...] + x


def _layer0(h, hist, w, *, tl):
    bsz, seq, _ = h.shape
    nl = seq // tl
    nt = bsz * nl
    cur = lambda s: jnp.minimum(s, nt - 1)
    prev = lambda s: jnp.maximum(s - 1, 0)
    return pl.pallas_call(
        functools.partial(_layer0_kernel, tl=tl, rc=64, nl=nl, nt=nt),
        grid=(nt + 1,),
        in_specs=[
            pl.BlockSpec((1, tl, D_MODEL), lambda s: (cur(s) // nl, cur(s) % nl, 0)),
            pl.BlockSpec((1, CONV_HIST, D_MODEL), lambda s: (0 if hist.shape[0] == 1 else cur(s) // nl, 0, 0)),
            _const_spec((1, D_MODEL)),
            _const_spec((D_MODEL, 2 * D_MODEL)),
            _const_spec((1, 2 * D_MODEL)),
            _const_spec((CONV_WIDTH, SUBLANES, D_MODEL)),
            _const_spec((1, D_MODEL)),
            _const_spec((1, D_MODEL)),
            _const_spec((1, D_MODEL)),
            _const_spec((D_MODEL, D_MODEL)),
            _const_spec((1, D_MODEL)),
            _const_spec((1, D_MODEL)),
            _const_spec((D_MODEL, D_FF)),
            _const_spec((D_MODEL, D_FF)),
            _const_spec((D_FF, D_MODEL)),
        ],
        out_specs=[
            pl.BlockSpec((1, tl, D_MODEL), lambda s: (prev(s) // nl, prev(s) % nl, 0)),
            pl.BlockSpec((1, CONV_HIST, D_MODEL), lambda s: (cur(s) // nl, 0, 0)),
        ],
        out_shape=[
            jax.ShapeDtypeStruct((bsz, seq, D_MODEL), F32),
            jax.ShapeDtypeStruct((bsz, CONV_HIST, D_MODEL), F32),
        ],
        scratch_shapes=[
            pltpu.VMEM((1, HIST_PAD + tl, D_MODEL), F32),
            pltpu.VMEM((1, tl, D_MODEL), F32),
            pltpu.VMEM((2, tl, D_MODEL), F32),
        ],
        compiler_params=_params(1),
        name="layer0",
    )(h, hist, w["gm0"], w["w_pw1"], w["b_pw1"], w["w_dw"], w["b_dw"], w["ln_g"], w["ln_b"], w["w_pw2"], w["b_pw2"],
      w["gf0"], w["wg0"], w["wu0"], w["wd0"])


def _softplus(x):
    return jnp.maximum(x, 0.0) + jnp.log(1.0 + jnp.exp(-jnp.abs(x)))


def _ssd_front_kernel(h_ref, ch_ref, gm_ref, wz_ref, wxa_ref, wxb_ref, wdt_ref, cw_ref, cb_ref, dtb_ref,
                      z_ref, xbc_ref, dt_ref, nc_ref, hist_ref, *, tb, tl):
    l = pl.program_id(1)
    m = tb * tl

    @pl.when(l == 0)
    def _():
        hist_ref[:, 0:SSD_HIST_PAD - SSD_CONV_HIST, :] = jnp.zeros((tb, SSD_HIST_PAD - SSD_CONV_HIST, CONV_DIM), F32)
        hist_ref[:, SSD_HIST_PAD - SSD_CONV_HIST:, :] = ch_ref[...]

    x = h_ref[...].reshape(m, D_MODEL)
    hn = _rms(x, gm_ref[...]).astype(BF16)
    dt_ref[...] = _softplus(_dot(hn, wdt_ref[...]) + dtb_ref[...]).reshape(tb, tl, LANES)
    row_in_group = lax.broadcasted_iota(jnp.int32, (tb, tl, FRONT_COLS), 1) % SUBLANES

    def conv_block(xn, lo):
        cs = slice(lo, lo + FRONT_COLS)
        xh = jnp.concatenate([hist_ref[:, :, cs], xn], axis=1)
        tap = lambda k: jnp.tile(cw_ref[k, :, cs], (tl // SUBLANES, 1))[None]
        acc = cb_ref[:, cs][None] + xn * tap(SSD_CONV_WIDTH - 1)
        for d in range(1, SSD_CONV_WIDTH):
            rot = _roll_groups_down(xh, d)
            acc = acc + jnp.where(row_in_group < d, rot[:, :tl], rot[:, SUBLANES:]) * tap(SSD_CONV_WIDTH - 1 - d)
        xbc_ref[:, :, cs] = _silu(acc).astype(xbc_ref.dtype)
        hist_ref[:, :, cs] = xn[:, tl - SSD_HIST_PAD:]

    def z_block(zn, lo):
        z_ref[:, :, lo:lo + FRONT_COLS] = zn.astype(z_ref.dtype)

    tasks = []
    z_cols = list(range(0, D_INNER, FRONT_COLS))
    for i, lo in enumerate(range(0, CONV_DIM, FRONT_COLS)):
        w_ref, off = (wxa_ref, lo) if lo < D_INNER else (wxb_ref, lo - D_INNER)
        tasks.append((w_ref, off, conv_block, lo))
        if i % 2 == 1:
            tasks.append((wz_ref, z_cols[i // 2], z_block, z_cols[i // 2]))
    pending = None
    for w_ref, off, finish, lo in tasks:
        res = _dot(hn, w_ref[:, off:off + FRONT_COLS]).reshape(tb, tl, FRONT_COLS)
        if pending is not None:
            pending[0](pending[1], pending[2])
        pending = (finish, res, lo)
    pending[0](pending[1], pending[2])

    @pl.when(l == pl.num_programs(1) - 1)
    def _():
        nc_ref[...] = hist_ref[:, SSD_HIST_PAD - SSD_CONV_HIST:, :]


def _ssd_front(h, ch, gm, w_in, wdt, cw, cb, dtb, *, tb, tl):
    bsz, seq, _ = h.shape
    act_dtype = BF16 if tl % (2 * SUBLANES) == 0 else F32
    w_blk = lambda j: pl.BlockSpec((D_MODEL, D_INNER), lambda *_: (0, j), pipeline_mode=pl.Buffered(1))
    return pl.pallas_call(
        functools.partial(_ssd_front_kernel, tb=tb, tl=tl),
        grid=(bsz // tb, seq // tl),
        in_specs=[
            pl.BlockSpec((tb, tl, D_MODEL), lambda b, l: (b, l, 0)),
            pl.BlockSpec((tb, SSD_CONV_HIST, CONV_DIM), lambda b, l: (0 if ch.shape[0] == 1 else b, 0, 0)),
            _const_spec((1, D_MODEL)),
            w_blk(0),
            w_blk(1),
            w_blk(2),
            _const_spec((D_MODEL, LANES)),
            _const_spec((SSD_CONV_WIDTH, SUBLANES, CONV_DIM)),
            _const_spec((1, CONV_DIM)),
            _const_spec((1, LANES)),
        ],
        out_specs=[
            pl.BlockSpec((tb, tl, D_INNER), lambda b, l: (b, l, 0)),
            pl.BlockSpec((tb, tl, CONV_DIM), lambda b, l: (b, l, 0)),
            pl.BlockSpec((tb, tl, LANES), lambda b, l: (b, l, 0)),
            pl.BlockSpec((tb, SSD_CONV_HIST, CONV_DIM), lambda b, l: (b, 0, 0)),
        ],
        out_shape=[
            jax.ShapeDtypeStruct((bsz, seq, D_INNER), act_dtype),
            jax.ShapeDtypeStruct((bsz, seq, CONV_DIM), act_dtype),
            jax.ShapeDtypeStruct((bsz, seq, LANES), F32),
            jax.ShapeDtypeStruct((bsz, SSD_CONV_HIST, CONV_DIM), F32),
        ],
        scratch_shapes=[pltpu.VMEM((tb, SSD_HIST_PAD, CONV_DIM), F32)],
        compiler_params=_params(2),
        name="ssd_front",
    )(h, ch, gm, w_in, w_in, w_in, wdt, cw, cb, dtb)


def _cumsum_rows(a, tri):
    a_hi = a.astype(BF16)
    r1 = a - a_hi.astype(F32)
    a_mid = r1.astype(BF16)
    a_lo = (r1 - a_mid.astype(F32)).astype(BF16)
    return _dot(tri, a_hi) + _dot(tri, a_mid) + _dot(tri, a_lo)


def _head_rows(mat, g, width):
    return jnp.concatenate(
        [jnp.broadcast_to(mat[HEADS_PER_GROUP * g + r:HEADS_PER_GROUP * g + r + 1, :], (HEAD_DIM, width))
         for r in range(HEADS_PER_GROUP)], axis=0)


def _scan_fns(xbc_ref, z_ref, st_ref, vn_ref, alog_ref, dske_ref, ng_ref, lc):
    row = lax.broadcasted_iota(jnp.int32, (lc, lc), 0)
    col = lax.broadcasted_iota(jnp.int32, (lc, lc), 1)
    causal = row >= col
    tri = jnp.where(causal, 1.0, 0.0).astype(BF16)
    lane_head = lax.broadcasted_iota(jnp.int32, (lc, GROUP_DIM), 1) // HEAD_DIM
    lane_lo = lax.broadcasted_iota(jnp.int32, (lc, LANES), 1) < HEAD_DIM
    neg_a = -jnp.exp(alog_ref[...])

    def group_dots(i, r0, g):
        b_off = D_INNER + g * D_STATE
        c_off = D_INNER + N_GROUPS * D_STATE + g * D_STATE
        bg = xbc_ref[i, r0:r0 + lc, b_off:b_off + D_STATE].astype(BF16)
        cg = xbc_ref[i, r0:r0 + lc, c_off:c_off + D_STATE].astype(BF16)
        cb = lax.dot_general(cg, bg, _NT, preferred_element_type=F32)
        hg = st_ref[i, g]
        yoff = lax.dot_general(cg, hg.astype(BF16), _NT, preferred_element_type=F32)
        return bg, cb, hg, yoff

    def decays(dt):
        acs = _cumsum_rows(dt * neg_a, tri)
        acs_t = acs.T
        dt_t = dt.T
        last_t = jnp.broadcast_to(acs_t[:, lc - 1:lc], (LANES, lc))
        c_t = acs_t - jnp.log(dt_t)
        coef_t = jnp.exp(last_t - acs_t) * dt_t
        cd = jnp.broadcast_to(jnp.exp(acs_t[:, lc - 1:lc]), (LANES, D_STATE))
        return acs, c_t, coef_t, cd

    def group_tail(i, r0, g, dots, dec):
        bg, cb, hg, yoff = dots
        acs, c_t, coef_t, cd = dec
        lo, hi = g * GROUP_DIM, (g + 1) * GROUP_DIM
        xg = xbc_ref[i, r0:r0 + lc, lo:hi].astype(F32)
        ws, acs_b = [], []
        for r in range(HEADS_PER_GROUP):
            h = HEADS_PER_GROUP * g + r
            ab = jnp.broadcast_to(acs[:, h:h + 1], (lc, LANES))
            acs_b.append(ab)
            ws.append((cb * jnp.exp(jnp.where(causal, ab - c_t[h:h + 1, :], -jnp.inf))).astype(BF16))
        x_bd = jnp.concatenate([jnp.where(lane_head == r, xg, 0.0).astype(BF16) for r in range(HEADS_PER_GROUP)],
                               axis=0)
        yd = _dot(jnp.concatenate(ws, axis=1), x_bd)
        e_b = jnp.exp(jnp.concatenate([jnp.where(lane_lo, acs_b[0], acs_b[1]),
                                       jnp.where(lane_lo, acs_b[2], acs_b[3])], axis=1))
        y = yd + e_b * yoff + dske_ref[:, lo:hi] * xg

        xcoef_t = (xg.T * _head_rows(coef_t, g, lc)).astype(BF16)
        st_ref[i, g] = _head_rows(cd, g, D_STATE) * hg + _dot(xcoef_t, bg)

        v = y * _silu(z_ref[i, r0:r0 + lc, lo:hi].astype(F32))
        vn = v * lax.rsqrt(jnp.mean(v * v, axis=-1, keepdims=True) + EPS) * ng_ref[:, lo:hi]
        vn_ref[i, r0:r0 + lc, lo:hi] = vn.astype(BF16)

    return group_dots, decays, group_tail


def _ssd_scan_kernel(xbc_ref, dt_ref, dtn_ref, z_ref, h_ref, h0_ref, alog_ref, dske_ref, ng_ref, wo_ref, o_ref, st_ref,
                     vn_ref, dec_ref, *, lc):
    c = pl.program_id(1)
    slot = c % 2
    group_dots, decays, group_tail = _scan_fns(xbc_ref, z_ref, st_ref, vn_ref, alog_ref, dske_ref, ng_ref, lc)

    @pl.when(c == 0)
    def _():
        st_ref[...] = h0_ref[...]
        for k, v in enumerate(decays(dt_ref[0])):
            dec_ref[0, k] = v

    half = D_INNER // 2
    dots = group_dots(0, 0, 0)
    dec = tuple(dec_ref[slot, k] for k in range(4))
    for g in range(N_GROUPS):
        nxt = group_dots(0, 0, g + 1) if g + 1 < N_GROUPS else None
        group_tail(0, 0, g, dots, dec)
        dots = nxt
        if g == 1:
            for k, v in enumerate(decays(dtn_ref[0])):
                dec_ref[1 - slot, k] = v
        if g == N_GROUPS // 2 - 1:
            o_ref[0] = h_ref[0] + _dot(vn_ref[0, :, :half], wo_ref[:half, :])
    o_ref[0] = o_ref[0] + _dot(vn_ref[0, :, half:], wo_ref[half:, :])


def _ssd_scan(xbc, dt, z, h, h0, alog, dske, ng, wo, *, lc):
    bsz, seq, _ = xbc.shape
    assert lc == LANES
    nb = 1
    last = seq // lc - 1
    shared_h0 = h0.shape[0] == 1
    return pl.pallas_call(
        functools.partial(_ssd_scan_kernel, lc=lc),
        grid=(bsz // nb, seq // lc),
        in_specs=[
            pl.BlockSpec((nb, lc, CONV_DIM), lambda b, c: (b, c, 0)),
            pl.BlockSpec((nb, lc, LANES), lambda b, c: (b, c, 0)),
            pl.BlockSpec((nb, lc, LANES), lambda b, c: (b, jnp.minimum(c + 1, last), 0)),
            pl.BlockSpec((nb, lc, D_INNER), lambda b, c: (b, c, 0)),
            pl.BlockSpec((nb, lc, D_MODEL), lambda b, c: (b, c, 0)),
            pl.BlockSpec((nb, N_GROUPS, GROUP_DIM, D_STATE), lambda b, c: (0 if shared_h0 else b, 0, 0, 0)),
            _const_spec((1, LANES)),
            _const_spec((1, D_INNER)),
            _const_spec((1, D_INNER)),
            _const_spec((D_INNER, D_MODEL)),
        ],
        out_specs=[
            pl.BlockSpec((nb, lc, D_MODEL), lambda b, c: (b, c, 0)),
            pl.BlockSpec((nb, N_GROUPS, GROUP_DIM, D_STATE), lambda b, c: (b, 0, 0, 0)),
        ],
        out_shape=[
            jax.ShapeDtypeStruct((bsz, seq, D_MODEL), F32),
            jax.ShapeDtypeStruct((bsz, N_GROUPS, GROUP_DIM, D_STATE), F32),
        ],
        scratch_shapes=[pltpu.VMEM((nb, lc, D_INNER), BF16), pltpu.VMEM((2, 4, LANES, LANES), F32)],
        compiler_params=_params(2),
        name="ssd_scan",
    )(xbc, dt, dt, z, h, h0, alog, dske, ng, wo)


def _ssd_step_kernel(x_ref, b_ref, c_ref, dt_ref, st_ref, alog_ref, dsk_ref, y_ref, so_ref, *, nb, ls):
    g = pl.program_id(1)
    rows = nb * ls
    shift = (LANES - HEADS_PER_GROUP * g) % LANES
    dt = pltpu.roll(dt_ref[...], shift, axis=1)
    alog = pltpu.roll(alog_ref[...], shift, axis=1)
    dsk = pltpu.roll(dsk_ref[...], shift, axis=1)
    a = dt * (-jnp.exp(alog))

    row = lax.broadcasted_iota(jnp.int32, (rows, rows), 0)
    col = lax.broadcasted_iota(jnp.int32, (rows, rows), 1)
    same = (row // ls) == (col // ls)
    causal = jnp.logical_and(same, row >= col)
    hi = lax.Precision.HIGHEST
    acs = jnp.dot(jnp.where(causal, 1.0, 0.0), a, preferred_element_type=F32, precision=hi)
    tot = jnp.dot(jnp.where(same, 1.0, 0.0), a, preferred_element_type=F32, precision=hi)
    acs_t = acs.T
    coef = jnp.exp(tot - acs) * dt
    eacs = jnp.exp(acs)
    etot = jnp.exp(tot)

    bg = b_ref[...].astype(BF16)
    cg = c_ref[...].astype(BF16)
    xg = x_ref[...]
    cb = lax.dot_general(cg, bg, _NT, preferred_element_type=F32)

    seq_of_col = lax.broadcasted_iota(jnp.int32, (GROUP_DIM, rows), 1) // ls
    yoff_t = jnp.zeros((GROUP_DIM, rows), F32)
    for b in range(nb):
        yb = lax.dot_general(st_ref[b, 0].astype(BF16), cg, _NT, preferred_element_type=F32)
        yoff_t = jnp.where(seq_of_col == b, yb, yoff_t)
    yoff = yoff_t.T

    ys, xcs = [], []
    for r in range(HEADS_PER_GROUP):
        decay = jnp.exp(jnp.where(causal, acs[:, r:r + 1] - acs_t[r:r + 1, :], -jnp.inf))
        w = (cb * decay).astype(BF16)
        xr = xg[:, r * HEAD_DIM:(r + 1) * HEAD_DIM]
        yd = _dot(w, (xr * dt[:, r:r + 1]).astype(BF16))
        ys.append(yd + eacs[:, r:r + 1] * yoff[:, r * HEAD_DIM:(r + 1) * HEAD_DIM] + dsk[:, r:r + 1] * xr)
        xcs.append(xr * coef[:, r:r + 1])
    y_ref[...] = jnp.concatenate(ys, axis=1)
    xcoef_t = jnp.concatenate(xcs, axis=1).T

    for b in range(nb):
        upd = _dot(jnp.where(seq_of_col == b, xcoef_t, 0.0).astype(BF16), bg)
        scale = jnp.concatenate(
            [jnp.broadcast_to(etot[b * ls:b * ls + 1, r:r + 1], (HEAD_DIM, D_STATE)) for r in range(HEADS_PER_GROUP)],
            axis=0)
        so_ref[b, 0] = scale * st_ref[b, 0] + upd


def _ssd_step(xbc, dt, st, alog, dsk, *, nb, ls):
    t = xbc.shape[0]
    n_seq = st.shape[0]
    rows = nb * ls
    b_blk0 = D_INNER // D_STATE
    c_blk0 = b_blk0 + N_GROUPS
    return pl.pallas_call(
        functools.partial(_ssd_step_kernel, nb=nb, ls=ls),
        grid=(n_seq // nb, N_GROUPS),
        in_specs=[
            pl.BlockSpec((rows, GROUP_DIM), lambda i, g: (i, g)),
            pl.BlockSpec((rows, D_STATE), lambda i, g: (i, b_blk0 + g)),
            pl.BlockSpec((rows, D_STATE), lambda i, g: (i, c_blk0 + g)),
            pl.BlockSpec((rows, LANES), lambda i, g: (i, 0)),
            pl.BlockSpec((nb, 1, GROUP_DIM, D_STATE), lambda i, g: (i, g, 0, 0)),
            _const_spec((1, LANES)),
            _const_spec((1, LANES)),
        ],
        out_specs=[
            pl.BlockSpec((rows, GROUP_DIM), lambda i, g: (i, g)),
            pl.BlockSpec((nb, 1, GROUP_DIM, D_STATE), lambda i, g: (i, g, 0, 0)),
        ],
        out_shape=[
            jax.ShapeDtypeStruct((t, D_INNER), F32),
            jax.ShapeDtypeStruct((n_seq, N_GROUPS, GROUP_DIM, D_STATE), F32),
        ],
        compiler_params=_params(2),
        name="ssd_step",
    )(xbc, xbc, xbc, dt, st, alog, dsk)


def _ssd_post_kernel(y_ref, z_ref, h_ref, ng_ref, wo_ref, o_ref):
    acc = h_ref[...]
    for g in range(N_GROUPS):
        lo, hi = g * GROUP_DIM, (g + 1) * GROUP_DIM
        v = y_ref[:, lo:hi] * _silu(z_ref[:, lo:hi])
        vn = v * lax.rsqrt(jnp.mean(v * v, axis=-1, keepdims=True) + EPS) * ng_ref[:, lo:hi]
        acc = acc + _dot(vn.astype(BF16), wo_ref[lo:hi, :])
    o_ref[...] = acc


def _ssd_post(y, z, h, ng, wo, *, tm):
    t = y.shape[0]
    return pl.pallas_call(
        _ssd_post_kernel,
        grid=(t // tm,),
        in_specs=[
            pl.BlockSpec((tm, D_INNER), lambda i: (i, 0)),
            pl.BlockSpec((tm, D_INNER), lambda i: (i, 0)),
            pl.BlockSpec((tm, D_MODEL), lambda i: (i, 0)),
            _const_spec((1, D_INNER)),
            _const_spec((D_INNER, D_MODEL)),
        ],
        out_specs=pl.BlockSpec((tm, D_MODEL), lambda i: (i, 0)),
        out_shape=jax.ShapeDtypeStruct((t, D_MODEL), F32),
        compiler_params=_params(1),
        name="ssd_post",
    )(y, z, h, ng, wo)


def _trunk(h, cc_hist, sc_hist, ssm, w, *, tb, tl, tm, short, tl_front=None):
    bsz, seq, _ = h.shape
    t = bsz * seq
    if tb == 1 and tl % 64 == 0:
        h2, new_cc = _layer0(h, cc_hist, w, tl=tl)
        h2 = h2.reshape(t, D_MODEL)
    else:
        h1, new_cc = _conformer(h, cc_hist, w["gm0"], w["w_pw1"], w["b_pw1"], w["w_dw"], w["b_dw"], w["ln_g"],
                                w["ln_b"], w["w_pw2"], w["b_pw2"], tb=tb, tl=tl)
        h2 = _ffn(h1.reshape(t, D_MODEL), w["gf0"], w["wg0"], w["wu0"], w["wd0"], w["g_final"], tm=tm, final=False)
    z, xbc, dt, new_sc = _ssd_front(h2.reshape(bsz, seq, D_MODEL), sc_hist, w["gm1"], w["w_in"], w["w_dt"],
                                    w["conv_w"], w["conv_b"], w["dt_bias"], tb=tb, tl=tl_front or tl)
    if short:
        y, new_ss = _ssd_step(xbc.reshape(t, CONV_DIM), dt.reshape(t, LANES), ssm, w["a_log"], w["d_skip"],
                              nb=LANES // seq, ls=seq)
        h3 = _ssd_post(y, z.reshape(t, D_INNER), h2, w["norm_g"], w["w_out"], tm=tm)
    else:
        pad = (-seq) % LANES
        rows = lambda v: jnp.pad(v, ((0, 0), (0, pad), (0, 0))) if pad else v
        h3, new_ss = _ssd_scan(rows(xbc), rows(dt), rows(z), rows(h2.reshape(bsz, seq, D_MODEL)), ssm,
                               w["a_log"], w["d_skip_lanes"], w["norm_g"], w["w_out"], lc=LANES)
        h3 = h3[:, :seq].reshape(t, D_MODEL)
    out = _ffn(h3, w["gf1"], w["wg1"], w["wu1"], w["wd1"], w["g_final"], tm=tm, final=True)
    return out.reshape(bsz, seq, D_MODEL), new_cc, new_sc, new_ss


def _pad_lanes(v):
    return jnp.pad(v.astype(F32), (0, LANES - v.shape[0])).reshape(1, LANES)


def kernel(x_prompt, x_sample, cache_conv, state_ssd_conv, state_ssm, meta_tokens, norm_mix, norm_ffn, norm_final, cf_w_pw1, cf_b_pw1, cf_w_dw, cf_b_dw, cf_ln_g, cf_ln_b, cf_w_pw2, cf_b_pw2, ssd_w_in, ssd_conv_w, ssd_conv_b, ssd_dt_bias, ssd_a_log, ssd_d, ssd_norm_g, ssd_w_out, ffn_w_gate, ffn_w_up, ffn_w_down):
    row = lambda v: v.astype(F32).reshape(1, -1)
    w_in = ssd_w_in[0]
    w = {
        "gm0": row(norm_mix[0]), "gm1": row(norm_mix[1]),
        "gf0": row(norm_ffn[0]), "gf1": row(norm_ffn[1]), "g_final": row(norm_final),
        "w_pw1": cf_w_pw1[0].astype(BF16), "b_pw1": row(cf_b_pw1[0]),
        "w_dw": jnp.broadcast_to(cf_w_dw[0][:, None, :], (CONV_WIDTH, SUBLANES, D_MODEL)),
        "b_dw": row(cf_b_dw[0]), "ln_g": row(cf_ln_g[0]), "ln_b": row(cf_ln_b[0]),
        "w_pw2": cf_w_pw2[0].astype(BF16), "b_pw2": row(cf_b_pw2[0]),
        "w_in": w_in.astype(BF16),
        "w_dt": jnp.pad(w_in[:, D_INNER + CONV_DIM:], ((0, 0), (0, LANES - N_HEADS))).astype(BF16),
        "conv_w": jnp.broadcast_to(ssd_conv_w[0][:, None, :], (SSD_CONV_WIDTH, SUBLANES, CONV_DIM)),
        "conv_b": row(ssd_conv_b[0]),
        "dt_bias": _pad_lanes(ssd_dt_bias[0]), "a_log": _pad_lanes(ssd_a_log[0]), "d_skip": _pad_lanes(ssd_d[0]),
        "d_skip_lanes": jnp.repeat(ssd_d[0].astype(F32), HEAD_DIM).reshape(1, D_INNER),
        "norm_g": row(ssd_norm_g[0]), "w_out": ssd_w_out[0].astype(BF16),
        "wg0": ffn_w_gate[0].astype(BF16), "wu0": ffn_w_up[0].astype(BF16), "wd0": ffn_w_down[0].astype(BF16),
        "wg1": ffn_w_gate[1].astype(BF16), "wu1": ffn_w_up[1].astype(BF16), "wd1": ffn_w_down[1].astype(BF16),
    }
    bp, seq, _ = x_prompt.shape
    bs, dseq, _ = x_sample.shape

    _, cc_m, sc_m, ss_m = _trunk(
        meta_tokens.astype(F32)[None], jnp.zeros((1, CONV_HIST, D_MODEL), F32),
        jnp.zeros((1, SSD_CONV_HIST, CONV_DIM), F32), jnp.zeros((1, N_GROUPS, GROUP_DIM, D_STATE), F32), w,
        tb=1, tl=N_META, tm=N_META, short=False)

    y_prompt, cc_p, sc_p, ss_p = _trunk(
        x_prompt, cc_m, sc_m,
        ss_m, w, tb=1, tl=256, tl_front=512, tm=512, short=False)

    y_sample, cc_s, sc_s, ss_s = _trunk(
        x_sample, cache_conv[0], state_ssd_conv[0], state_ssm[0].reshape(bs, N_GROUPS, GROUP_DIM, D_STATE), w,
        tb=32, tl=dseq, tm=512, short=True)

    unpack_ss = lambda v: v.reshape(1, v.shape[0], N_HEADS, HEAD_DIM, D_STATE)
    return (y_prompt, y_sample, cc_p[None], cc_s[None], sc_p[None], sc_s[None], unpack_ss(ss_p), unpack_ss(ss_s))
```

```python
import functools

import jax
import jax.numpy as jnp
from jax import lax
from jax.experimental import pallas as pl
from jax.experimental.pallas import tpu as pltpu

D_MODEL = 1024
N_META = 16
CONV_WIDTH = 31
CONV_HIST = CONV_WIDTH - 1
D_INNER = 2048
HEAD_DIM = 64
N_HEADS = 32
N_GROUPS = 8
HEADS_PER_GROUP = 4
GROUP_DIM = HEADS_PER_GROUP * HEAD_DIM
D_STATE = 128
SSD_CONV_WIDTH = 4
SSD_CONV_HIST = SSD_CONV_WIDTH - 1
CONV_DIM = D_INNER + 2 * N_GROUPS * D_STATE
D_FF = 2816
EPS = 1e-6

LANES = 128
SUBLANES = 8
HIST_PAD = 32
SSD_HIST_PAD = 8
FRONT_COLS = 256
SCAN_CHUNKS = 4
FFN_COLS = 512
VMEM_LIMIT = 56 * 1024 * 1024

F32 = jnp.float32
BF16 = jnp.bfloat16
_NT = (((1,), (1,)), ((), ()))


def _const_spec(shape):
    return pl.BlockSpec(shape, lambda *_: (0,) * len(shape), pipeline_mode=pl.Buffered(1))


def _params(n_axes):
    return pltpu.CompilerParams(dimension_semantics=("arbitrary",) * n_axes, vmem_limit_bytes=VMEM_LIMIT)


def _rms(x, g):
    return x * lax.rsqrt(jnp.mean(x * x, axis=-1, keepdims=True) + EPS) * g


def _sigmoid(x):
    return 1.0 / (1.0 + jnp.exp(-x))


def _silu(x):
    return x * _sigmoid(x)


def _dot(a, b):
    return jnp.dot(a, b, preferred_element_type=F32)


def _roll_groups_down(x, d):
    b, n, c = x.shape
    return pltpu.roll(x.reshape(b * n // SUBLANES, SUBLANES, c), d, axis=1).reshape(b, n, c)


def _shift_rows_up(x, s):
    n = x.shape[0] - SUBLANES
    rot = [pltpu.roll(x[j:j + SUBLANES], SUBLANES - s, axis=0) for j in range(0, n + SUBLANES, SUBLANES)]
    keep = lax.broadcasted_iota(jnp.int32, (SUBLANES, x.shape[1]), 0) < SUBLANES - s
    return jnp.concatenate([jnp.where(keep, rot[j], rot[j + 1]) for j in range(n // SUBLANES)], axis=0)


def _dwconv(src_ref, w_ref, bias_ref, dst_ref, *, tb, tl, rc, cc, taps, base, act, cols=None, static=False):
    n_rc = tl // rc
    c_lo, c_hi = cols if cols is not None else (0, src_ref.shape[-1])
    by_shift = {}
    for k in range(taps):
        by_shift.setdefault((base + k) % SUBLANES, []).append(k)

    def piece(b, r0, c0):
        acc = jnp.broadcast_to(bias_ref[:, c0:c0 + cc], (rc, cc))
        for s, ks in sorted(by_shift.items()):
            n_rows = rc + SUBLANES if s else rc
            part = None
            for k in ks:
                q = (base + k) // SUBLANES
                blk = src_ref[b, pl.ds(r0 + SUBLANES * q, n_rows), c0:c0 + cc]
                wk = jnp.tile(w_ref[k, :, c0:c0 + cc], (n_rows // SUBLANES, 1))
                part = blk * wk if part is None else part + blk * wk
            acc = acc + (_shift_rows_up(part, s) if s else part)
        dst_ref[b, pl.ds(r0, rc), c0:c0 + cc] = act(acc)

    def chunk(b, r0):
        for c0 in range(c_lo, c_hi, cc):
            piece(b, r0, c0)

    if static == "deferred":
        return [functools.partial(piece, b, j * rc, c0)
                for b in range(tb) for j in range(n_rc) for c0 in range(c_lo, c_hi, cc)]
    if static:
        for b in range(tb):
            for j in range(n_rc):
                chunk(b, j * rc)
    else:
        def body(i, carry):
            chunk(i // n_rc, pl.multiple_of((i % n_rc) * rc, SUBLANES))
            return carry

        lax.fori_loop(0, tb * n_rc, body, 0)


def _conformer_kernel(h_ref, hist_ref, gm_ref, w1_ref, b1_ref, wdw_ref, bdw_ref, lng_ref, lnb_ref, w2_ref, b2_ref,
                      out_ref, nh_ref, gh_ref, c_ref, *, tb, tl, rc):
    l = pl.program_id(1)
    m = tb * tl

    @pl.when(l == 0)
    def _():
        gh_ref[:, 0:HIST_PAD - CONV_HIST, :] = jnp.zeros((tb, HIST_PAD - CONV_HIST, D_MODEL), F32)
        gh_ref[:, HIST_PAD - CONV_HIST:HIST_PAD, :] = hist_ref[...]

    x = h_ref[...].reshape(m, D_MODEL)
    hn = _rms(x, gm_ref[...]).astype(BF16)
    u = _dot(hn, w1_ref[...]) + b1_ref[...]
    g = u[:, :D_MODEL] * _sigmoid(u[:, D_MODEL:])
    gh_ref[:, HIST_PAD:HIST_PAD + tl, :] = g.reshape(tb, tl, D_MODEL)

    _dwconv(gh_ref, wdw_ref, bdw_ref, c_ref, tb=tb, tl=tl, rc=rc, cc=LANES if rc >= 64 else 4 * LANES,
            taps=CONV_WIDTH, base=HIST_PAD - CONV_HIST, act=lambda v: v)

    @pl.when(l == pl.num_programs(1) - 1)
    def _():
        nh_ref[...] = gh_ref[:, tl + HIST_PAD - CONV_HIST:tl + HIST_PAD, :]

    gh_ref[:, 0:HIST_PAD, :] = gh_ref[:, tl:tl + HIST_PAD, :]

    c = c_ref[...].reshape(m, D_MODEL)
    mu = jnp.mean(c, axis=-1, keepdims=True)
    xc = c - mu
    cn = xc * lax.rsqrt(jnp.mean(xc * xc, axis=-1, keepdims=True) + EPS) * lng_ref[...] + lnb_ref[...]
    cn = _silu(cn).astype(BF16)
    y = _dot(cn, w2_ref[...]) + b2_ref[...] + x
    out_ref[...] = y.reshape(tb, tl, D_MODEL)


def _conformer(h, hist, gm, w1, b1, wdw, bdw, lng, lnb, w2, b2, *, tb, tl):
    bsz, seq, _ = h.shape
    rc = 64 if tl % 64 == 0 else (16 if tl % 16 == 0 else SUBLANES)
    kern = functools.partial(_conformer_kernel, tb=tb, tl=tl, rc=rc)
    return pl.pallas_call(
        kern,
        grid=(bsz // tb, seq // tl),
        in_specs=[
            pl.BlockSpec((tb, tl, D_MODEL), lambda b, l: (b, l, 0)),
            pl.BlockSpec((tb, CONV_HIST, D_MODEL), lambda b, l: (b, 0, 0)),
            _const_spec((1, D_MODEL)),
            _const_spec((D_MODEL, 2 * D_MODEL)),
            _const_spec((1, 2 * D_MODEL)),
            _const_spec((CONV_WIDTH, SUBLANES, D_MODEL)),
            _const_spec((1, D_MODEL)),
            _const_spec((1, D_MODEL)),
            _const_spec((1, D_MODEL)),
            _const_spec((D_MODEL, D_MODEL)),
            _const_spec((1, D_MODEL)),
        ],
        out_specs=[
            pl.BlockSpec((tb, tl, D_MODEL), lambda b, l: (b, l, 0)),
            pl.BlockSpec((tb, CONV_HIST, D_MODEL), lambda b, l: (b, 0, 0)),
        ],
        out_shape=[
            jax.ShapeDtypeStruct((bsz, seq, D_MODEL), F32),
            jax.ShapeDtypeStruct((bsz, CONV_HIST, D_MODEL), F32),
        ],
        scratch_shapes=[
            pltpu.VMEM((tb, HIST_PAD + tl, D_MODEL), F32),
            pltpu.VMEM((tb, tl, D_MODEL), F32),
        ],
        compiler_params=_params(2),
        name="conformer",
    )(h, hist, gm, w1, b1, wdw, bdw, lng, lnb, w2, b2)


def _ffn_kernel(x_ref, g_ref, wg_ref, wu_ref, wd_ref, gf_ref, o_ref, *, final, cols):
    x = x_ref[...]
    hn = _rms(x, g_ref[...]).astype(BF16)
    y = x
    for lo in range(0, D_FF, cols):
        hi = min(lo + cols, D_FF)
        t = (_silu(_dot(hn, wg_ref[:, lo:hi])) * _dot(hn, wu_ref[:, lo:hi])).astype(BF16)
        y = y + _dot(t, wd_ref[lo:hi, :])
        if hi < D_FF:
            o_ref[...] = y
            y = o_ref[...]
    if final:
        y = _rms(y, gf_ref[...])
    o_ref[...] = y


def _ffn(x, g, wg, wu, wd, gf, *, tm, final):
    t = x.shape[0]
    cols = D_FF if tm <= 512 else FFN_COLS
    return pl.pallas_call(
        functools.partial(_ffn_kernel, final=final, cols=cols),
        grid=(t // tm,),
        in_specs=[
            pl.BlockSpec((tm, D_MODEL), lambda i: (i, 0)),
            _const_spec((1, D_MODEL)),
            _const_spec((D_MODEL, D_FF)),
            _const_spec((D_MODEL, D_FF)),
            _const_spec((D_FF, D_MODEL)),
            _const_spec((1, D_MODEL)),
        ],
        out_specs=pl.BlockSpec((tm, D_MODEL), lambda i: (i, 0)),
        out_shape=jax.ShapeDtypeStruct((t, D_MODEL), F32),
        compiler_params=_params(1),
        name="ffn_final" if final else "ffn",
    )(x, g, wg, wu, wd, gf)


def _layer0_kernel(h_ref, hist_ref, gm_ref, w1_ref, b1_ref, wdw_ref, bdw_ref, lng_ref, lnb_ref, w2_ref, b2_ref,
                   gf_ref, wg_ref, wu_ref, wd_ref, out_ref, nh_ref, gh_ref, c_ref, h1_ref, *, tl, rc, nl, nt):
    s = pl.program_id(0)
    slot = s % 2
    tile = jnp.minimum(s, nt - 1)

    @pl.when(s == 0)
    def _():
        h1_ref[1] = jnp.zeros((tl, D_MODEL), F32)

    @pl.when(tile % nl == 0)
    def _():
        gh_ref[:, 0:HIST_PAD - CONV_HIST, :] = jnp.zeros((1, HIST_PAD - CONV_HIST, D_MODEL), F32)
        gh_ref[:, HIST_PAD - CONV_HIST:HIST_PAD, :] = hist_ref[...]

    x1 = h1_ref[1 - slot]
    hn1 = _rms(x1, gf_ref[...]).astype(BF16)
    out_ref[0] = x1

    def ffn_block(lo, hi):
        t = (_silu(_dot(hn1, wg_ref[:, lo:hi])) * _dot(hn1, wu_ref[:, lo:hi])).astype(BF16)
        out_ref[0] = out_ref[0] + _dot(t, wd_ref[lo:hi, :])

    x = h_ref[0]
    hn = _rms(x, gm_ref[...]).astype(BF16)
    u = _dot(hn, w1_ref[...]) + b1_ref[...]
    gh_ref[0, HIST_PAD:HIST_PAD + tl, :] = u[:, :D_MODEL] * _sigmoid(u[:, D_MODEL:])
    pieces = _dwconv(gh_ref, wdw_ref, bdw_ref, c_ref, tb=1, tl=tl, rc=rc, cc=LANES, taps=CONV_WIDTH,
                     base=HIST_PAD - CONV_HIST, act=lambda v: v, static="deferred")
    blocks = [(lo, min(lo + FFN_COLS, D_FF)) for lo in range(0, D_FF, FFN_COLS)]
    per_block = -(-len(pieces) // len(blocks))
    @pl.when(s >= 0)
    def _():
        for j, (lo, hi) in enumerate(blocks):
            ffn_block(lo, hi)
            for p in pieces[j * per_block:(j + 1) * per_block]:
                p()
    nh_ref[...] = gh_ref[:, tl + HIST_PAD - CONV_HIST:tl + HIST_PAD, :]
    gh_ref[:, 0:HIST_PAD, :] = gh_ref[:, tl:tl + HIST_PAD, :]
    c = c_ref[0]
    xc = c - jnp.mean(c, axis=-1, keepdims=True)
    cn = xc * lax.rsqrt(jnp.mean(xc * xc, axis=-1, keepdims=True) + EPS) * lng_ref[...] + lnb_ref[...]
    h1_ref[slot] = _dot(_silu(cn).astype(BF16), w2_ref[...]) + b2_ref[...] + x


def _layer0(h, hist, w, *, tl):
    bsz, seq, _ = h.shape
    nl = seq // tl
    nt = bsz * nl
    cur = lambda s: jnp.minimum(s, nt - 1)
    prev = lambda s: jnp.maximum(s - 1, 0)
    return pl.pallas_call(
        functools.partial(_layer0_kernel, tl=tl, rc=64, nl=nl, nt=nt),
        grid=(nt + 1,),
        in_specs=[
            pl.BlockSpec((1, tl, D_MODEL), lambda s: (cur(s) // nl, cur(s) % nl, 0)),
            pl.BlockSpec((1, CONV_HIST, D_MODEL), lambda s: (0 if hist.shape[0] == 1 else cur(s) // nl, 0, 0)),
            _const_spec((1, D_MODEL)),
            _const_spec((D_MODEL, 2 * D_MODEL)),
            _const_spec((1, 2 * D_MODEL)),
            _const_spec((CONV_WIDTH, SUBLANES, D_MODEL)),
            _const_spec((1, D_MODEL)),
            _const_spec((1, D_MODEL)),
            _const_spec((1, D_MODEL)),
            _const_spec((D_MODEL, D_MODEL)),
            _const_spec((1, D_MODEL)),
            _const_spec((1, D_MODEL)),
            _const_spec((D_MODEL, D_FF)),
            _const_spec((D_MODEL, D_FF)),
            _const_spec((D_FF, D_MODEL)),
        ],
        out_specs=[
            pl.BlockSpec((1, tl, D_MODEL), lambda s: (prev(s) // nl, prev(s) % nl, 0)),
            pl.BlockSpec((1, CONV_HIST, D_MODEL), lambda s: (cur(s) // nl, 0, 0)),
        ],
        out_shape=[
            jax.ShapeDtypeStruct((bsz, seq, D_MODEL), F32),
            jax.ShapeDtypeStruct((bsz, CONV_HIST, D_MODEL), F32),
        ],
        scratch_shapes=[
            pltpu.VMEM((1, HIST_PAD + tl, D_MODEL), F32),
            pltpu.VMEM((1, tl, D_MODEL), F32),
            pltpu.VMEM((2, tl, D_MODEL), F32),
        ],
        compiler_params=_params(1),
        name="layer0",
    )(h, hist, w["gm0"], w["w_pw1"], w["b_pw1"], w["w_dw"], w["b_dw"], w["ln_g"], w["ln_b"], w["w_pw2"], w["b_pw2"],
      w["gf0"], w["wg0"], w["wu0"], w["wd0"])


def _softplus(x):
    return jnp.maximum(x, 0.0) + jnp.log(1.0 + jnp.exp(-jnp.abs(x)))


def _ssd_front_kernel(h_ref, ch_ref, gm_ref, wz_ref, wxa_ref, wxb_ref, wdt_ref, cw_ref, cb_ref, dtb_ref,
                      z_ref, xbc_ref, dt_ref, nc_ref, hist_ref, *, tb, tl):
    l = pl.program_id(1)
    m = tb * tl

    @pl.when(l == 0)
    def _():
        hist_ref[:, 0:SSD_HIST_PAD - SSD_CONV_HIST, :] = jnp.zeros((tb, SSD_HIST_PAD - SSD_CONV_HIST, CONV_DIM), F32)
        hist_ref[:, SSD_HIST_PAD - SSD_CONV_HIST:, :] = ch_ref[...]

    x = h_ref[...].reshape(m, D_MODEL)
    hn = _rms(x, gm_ref[...]).astype(BF16)
    dt_ref[...] = _softplus(_dot(hn, wdt_ref[...]) + dtb_ref[...]).reshape(tb, tl, LANES)
    row_in_group = lax.broadcasted_iota(jnp.int32, (tb, tl, FRONT_COLS), 1) % SUBLANES

    def conv_block(xn, lo):
        cs = slice(lo, lo + FRONT_COLS)
        xh = jnp.concatenate([hist_ref[:, :, cs], xn], axis=1)
        tap = lambda k: jnp.tile(cw_ref[k, :, cs], (tl // SUBLANES, 1))[None]
        acc = cb_ref[:, cs][None] + xn * tap(SSD_CONV_WIDTH - 1)
        for d in range(1, SSD_CONV_WIDTH):
            rot = _roll_groups_down(xh, d)
            acc = acc + jnp.where(row_in_group < d, rot[:, :tl], rot[:, SUBLANES:]) * tap(SSD_CONV_WIDTH - 1 - d)
        xbc_ref[:, :, cs] = _silu(acc).astype(xbc_ref.dtype)
        hist_ref[:, :, cs] = xn[:, tl - SSD_HIST_PAD:]

    def z_block(zn, lo):
        z_ref[:, :, lo:lo + FRONT_COLS] = zn.astype(z_ref.dtype)

    tasks = []
    z_cols = list(range(0, D_INNER, FRONT_COLS))
    for i, lo in enumerate(range(0, CONV_DIM, FRONT_COLS)):
        w_ref, off = (wxa_ref, lo) if lo < D_INNER else (wxb_ref, lo - D_INNER)
        tasks.append((w_ref, off, conv_block, lo))
        if i % 2 == 1:
            tasks.append((wz_ref, z_cols[i // 2], z_block, z_cols[i // 2]))
    pending = None
    for w_ref, off, finish, lo in tasks:
        res = _dot(hn, w_ref[:, off:off + FRONT_COLS]).reshape(tb, tl, FRONT_COLS)
        if pending is not None:
            pending[0](pending[1], pending[2])
        pending = (finish, res, lo)
    pending[0](pending[1], pending[2])

    @pl.when(l == pl.num_programs(1) - 1)
    def _():
        nc_ref[...] = hist_ref[:, SSD_HIST_PAD - SSD_CONV_HIST:, :]


def _ssd_front(h, ch, gm, w_in, wdt, cw, cb, dtb, *, tb, tl):
    bsz, seq, _ = h.shape
    act_dtype = BF16 if tl % (2 * SUBLANES) == 0 else F32
    w_blk = lambda j: pl.BlockSpec((D_MODEL, D_INNER), lambda *_: (0, j), pipeline_mode=pl.Buffered(1))
    return pl.pallas_call(
        functools.partial(_ssd_front_kernel, tb=tb, tl=tl),
        grid=(bsz // tb, seq // tl),
        in_specs=[
            pl.BlockSpec((tb, tl, D_MODEL), lambda b, l: (b, l, 0)),
            pl.BlockSpec((tb, SSD_CONV_HIST, CONV_DIM), lambda b, l: (0 if ch.shape[0] == 1 else b, 0, 0)),
            _const_spec((1, D_MODEL)),
            w_blk(0),
            w_blk(1),
            w_blk(2),
            _const_spec((D_MODEL, LANES)),
            _const_spec((SSD_CONV_WIDTH, SUBLANES, CONV_DIM)),
            _const_spec((1, CONV_DIM)),
            _const_spec((1, LANES)),
        ],
        out_specs=[
            pl.BlockSpec((tb, tl, D_INNER), lambda b, l: (b, l, 0)),
            pl.BlockSpec((tb, tl, CONV_DIM), lambda b, l: (b, l, 0)),
            pl.BlockSpec((tb, tl, LANES), lambda b, l: (b, l, 0)),
            pl.BlockSpec((tb, SSD_CONV_HIST, CONV_DIM), lambda b, l: (b, 0, 0)),
        ],
        out_shape=[
            jax.ShapeDtypeStruct((bsz, seq, D_INNER), act_dtype),
            jax.ShapeDtypeStruct((bsz, seq, CONV_DIM), act_dtype),
            jax.ShapeDtypeStruct((bsz, seq, LANES), F32),
            jax.ShapeDtypeStruct((bsz, SSD_CONV_HIST, CONV_DIM), F32),
        ],
        scratch_shapes=[pltpu.VMEM((tb, SSD_HIST_PAD, CONV_DIM), F32)],
        compiler_params=_params(2),
        name="ssd_front",
    )(h, ch, gm, w_in, w_in, w_in, wdt, cw, cb, dtb)


def _cumsum_rows(a, tri):
    a_hi = a.astype(BF16)
    r1 = a - a_hi.astype(F32)
    a_mid = r1.astype(BF16)
    a_lo = (r1 - a_mid.astype(F32)).astype(BF16)
    return _dot(tri, a_hi) + _dot(tri, a_mid) + _dot(tri, a_lo)


def _head_rows(mat, g, width):
    return jnp.concatenate(
        [jnp.broadcast_to(mat[HEADS_PER_GROUP * g + r:HEADS_PER_GROUP * g + r + 1, :], (HEAD_DIM, width))
         for r in range(HEADS_PER_GROUP)], axis=0)


def _scan_fns(xbc_ref, z_ref, st_ref, vn_ref, alog_ref, dske_ref, ng_ref, lc):
    row = lax.broadcasted_iota(jnp.int32, (lc, lc), 0)
    col = lax.broadcasted_iota(jnp.int32, (lc, lc), 1)
    causal = row >= col
    tri = jnp.where(causal, 1.0, 0.0).astype(BF16)
    lane_head = lax.broadcasted_iota(jnp.int32, (lc, GROUP_DIM), 1) // HEAD_DIM
    lane_lo = lax.broadcasted_iota(jnp.int32, (lc, LANES), 1) < HEAD_DIM
    neg_a = -jnp.exp(alog_ref[...])

    def group_dots(i, r0, g):
        b_off = D_INNER + g * D_STATE
        c_off = D_INNER + N_GROUPS * D_STATE + g * D_STATE
        bg = xbc_ref[i, r0:r0 + lc, b_off:b_off + D_STATE].astype(BF16)
        cg = xbc_ref[i, r0:r0 + lc, c_off:c_off + D_STATE].astype(BF16)
        cb = lax.dot_general(cg, bg, _NT, preferred_element_type=F32)
        hg = st_ref[i, g]
        yoff = lax.dot_general(cg, hg.astype(BF16), _NT, preferred_element_type=F32)
        return bg, cb, hg, yoff

    def decays(dt):
        acs = _cumsum_rows(dt * neg_a, tri)
        acs_t = acs.T
        dt_t = dt.T
        last_t = jnp.broadcast_to(acs_t[:, lc - 1:lc], (LANES, lc))
        c_t = acs_t - jnp.log(dt_t)
        coef_t = jnp.exp(last_t - acs_t) * dt_t
        cd = jnp.broadcast_to(jnp.exp(acs_t[:, lc - 1:lc]), (LANES, D_STATE))
        return acs, c_t, coef_t, cd

    def group_tail(i, r0, g, dots, dec):
        bg, cb, hg, yoff = dots
        acs, c_t, coef_t, cd = dec
        lo, hi = g * GROUP_DIM, (g + 1) * GROUP_DIM
        xg = xbc_ref[i, r0:r0 + lc, lo:hi].astype(F32)
        ws, acs_b = [], []
        for r in range(HEADS_PER_GROUP):
            h = HEADS_PER_GROUP * g + r
            ab = jnp.broadcast_to(acs[:, h:h + 1], (lc, LANES))
            acs_b.append(ab)
            ws.append((cb * jnp.exp(jnp.where(causal, ab - c_t[h:h + 1, :], -jnp.inf))).astype(BF16))
        x_bd = jnp.concatenate([jnp.where(lane_head == r, xg, 0.0).astype(BF16) for r in range(HEADS_PER_GROUP)],
                               axis=0)
        yd = _dot(jnp.concatenate(ws, axis=1), x_bd)
        e_b = jnp.exp(jnp.concatenate([jnp.where(lane_lo, acs_b[0], acs_b[1]),
                                       jnp.where(lane_lo, acs_b[2], acs_b[3])], axis=1))
        y = yd + e_b * yoff + dske_ref[:, lo:hi] * xg

        xcoef_t = (xg.T * _head_rows(coef_t, g, lc)).astype(BF16)
        st_ref[i, g] = _head_rows(cd, g, D_STATE) * hg + _dot(xcoef_t, bg)

        v = y * _silu(z_ref[i, r0:r0 + lc, lo:hi].astype(F32))
        vn = v * lax.rsqrt(jnp.mean(v * v, axis=-1, keepdims=True) + EPS) * ng_ref[:, lo:hi]
        vn_ref[i, r0:r0 + lc, lo:hi] = vn.astype(BF16)

    return group_dots, decays, group_tail


def _ssd_scan_kernel(xbc_ref, dt_ref, dtn_ref, z_ref, h_ref, h0_ref, alog_ref, dske_ref, ng_ref, wo_ref, o_ref, st_ref,
                     vn_ref, dec_ref, *, lc, n_chunks):
    c = pl.program_id(1)
    first_slot = 0 if n_chunks % 2 == 0 else c % 2
    group_dots, decays, group_tail = _scan_fns(xbc_ref, z_ref, st_ref, vn_ref, alog_ref, dske_ref, ng_ref, lc)

    @pl.when(c == 0)
    def _():
        st_ref[...] = h0_ref[...]
        for k, v in enumerate(decays(dt_ref[0, 0:lc, :])):
            dec_ref[0, k] = v

    half = D_INNER // 2
    for ci in range(n_chunks):
        r0 = ci * lc
        slot = (first_slot + ci) % 2
        dots = group_dots(0, r0, 0)
        dec = tuple(dec_ref[slot, k] for k in range(4))
        for g in range(N_GROUPS):
            nxt = group_dots(0, r0, g + 1) if g + 1 < N_GROUPS else None
            group_tail(0, r0, g, dots, dec)
            dots = nxt
            if g == 1:
                next_dt = dt_ref[0, r0 + lc:r0 + 2 * lc, :] if ci + 1 < n_chunks else dtn_ref[0]
                for k, v in enumerate(decays(next_dt)):
                    dec_ref[1 - slot, k] = v
            if g == N_GROUPS // 2 - 1:
                o_ref[0, r0:r0 + lc, :] = h_ref[0, r0:r0 + lc, :] + _dot(vn_ref[0, r0:r0 + lc, :half], wo_ref[:half, :])
        o_ref[0, r0:r0 + lc, :] = o_ref[0, r0:r0 + lc, :] + _dot(vn_ref[0, r0:r0 + lc, half:], wo_ref[half:, :])


def _ssd_scan(xbc, dt, z, h, h0, alog, dske, ng, wo, *, lc):
    bsz, seq, _ = xbc.shape
    assert lc == LANES
    nb = 1
    n_chunks = SCAN_CHUNKS if seq % (SCAN_CHUNKS * lc) == 0 else 1
    rows = n_chunks * lc
    last = seq // lc - 1
    shared_h0 = h0.shape[0] == 1
    return pl.pallas_call(
        functools.partial(_ssd_scan_kernel, lc=lc, n_chunks=n_chunks),
        grid=(bsz // nb, seq // rows),
        in_specs=[
            pl.BlockSpec((nb, rows, CONV_DIM), lambda b, c: (b, c, 0)),
            pl.BlockSpec((nb, rows, LANES), lambda b, c: (b, c, 0)),
            pl.BlockSpec((nb, lc, LANES), lambda b, c: (b, jnp.minimum((c + 1) * n_chunks, last), 0)),
            pl.BlockSpec((nb, rows, D_INNER), lambda b, c: (b, c, 0)),
            pl.BlockSpec((nb, rows, D_MODEL), lambda b, c: (b, c, 0)),
            pl.BlockSpec((nb, N_GROUPS, GROUP_DIM, D_STATE), lambda b, c: (0 if shared_h0 else b, 0, 0, 0)),
            _const_spec((1, LANES)),
            _const_spec((1, D_INNER)),
            _const_spec((1, D_INNER)),
            _const_spec((D_INNER, D_MODEL)),
        ],
        out_specs=[
            pl.BlockSpec((nb, rows, D_MODEL), lambda b, c: (b, c, 0)),
            pl.BlockSpec((nb, N_GROUPS, GROUP_DIM, D_STATE), lambda b, c: (b, 0, 0, 0)),
        ],
        out_shape=[
            jax.ShapeDtypeStruct((bsz, seq, D_MODEL), F32),
            jax.ShapeDtypeStruct((bsz, N_GROUPS, GROUP_DIM, D_STATE), F32),
        ],
        scratch_shapes=[pltpu.VMEM((nb, rows, D_INNER), BF16), pltpu.VMEM((2, 4, LANES, LANES), F32)],
        compiler_params=_params(2),
        name="ssd_scan",
    )(xbc, dt, dt, z, h, h0, alog, dske, ng, wo)


def _ssd_step_kernel(x_ref, b_ref, c_ref, dt_ref, st_ref, alog_ref, dsk_ref, y_ref, so_ref, *, nb, ls):
    g = pl.program_id(1)
    rows = nb * ls
    shift = (LANES - HEADS_PER_GROUP * g) % LANES
    dt = pltpu.roll(dt_ref[...], shift, axis=1)
    alog = pltpu.roll(alog_ref[...], shift, axis=1)
    dsk = pltpu.roll(dsk_ref[...], shift, axis=1)
    a = dt * (-jnp.exp(alog))

    row = lax.broadcasted_iota(jnp.int32, (rows, rows), 0)
    col = lax.broadcasted_iota(jnp.int32, (rows, rows), 1)
    same = (row // ls) == (col // ls)
    causal = jnp.logical_and(same, row >= col)
    hi = lax.Precision.HIGHEST
    acs = jnp.dot(jnp.where(causal, 1.0, 0.0), a, preferred_element_type=F32, precision=hi)
    tot = jnp.dot(jnp.where(same, 1.0, 0.0), a, preferred_element_type=F32, precision=hi)
    acs_t = acs.T
    coef = jnp.exp(tot - acs) * dt
    eacs = jnp.exp(acs)
    etot = jnp.exp(tot)

    bg = b_ref[...].astype(BF16)
    cg = c_ref[...].astype(BF16)
    xg = x_ref[...]
    cb = lax.dot_general(cg, bg, _NT, preferred_element_type=F32)

    seq_of_col = lax.broadcasted_iota(jnp.int32, (GROUP_DIM, rows), 1) // ls
    yoff_t = jnp.zeros((GROUP_DIM, rows), F32)
    for b in range(nb):
        yb = lax.dot_general(st_ref[b, 0].astype(BF16), cg, _NT, preferred_element_type=F32)
        yoff_t = jnp.where(seq_of_col == b, yb, yoff_t)
    yoff = yoff_t.T

    ys, xcs = [], []
    for r in range(HEADS_PER_GROUP):
        decay = jnp.exp(jnp.where(causal, acs[:, r:r + 1] - acs_t[r:r + 1, :], -jnp.inf))
        w = (cb * decay).astype(BF16)
        xr = xg[:, r * HEAD_DIM:(r + 1) * HEAD_DIM]
        yd = _dot(w, (xr * dt[:, r:r + 1]).astype(BF16))
        ys.append(yd + eacs[:, r:r + 1] * yoff[:, r * HEAD_DIM:(r + 1) * HEAD_DIM] + dsk[:, r:r + 1] * xr)
        xcs.append(xr * coef[:, r:r + 1])
    y_ref[...] = jnp.concatenate(ys, axis=1)
    xcoef_t = jnp.concatenate(xcs, axis=1).T

    for b in range(nb):
        upd = _dot(jnp.where(seq_of_col == b, xcoef_t, 0.0).astype(BF16), bg)
        scale = jnp.concatenate(
            [jnp.broadcast_to(etot[b * ls:b * ls + 1, r:r + 1], (HEAD_DIM, D_STATE)) for r in range(HEADS_PER_GROUP)],
            axis=0)
        so_ref[b, 0] = scale * st_ref[b, 0] + upd


def _ssd_step(xbc, dt, st, alog, dsk, *, nb, ls):
    t = xbc.shape[0]
    n_seq = st.shape[0]
    rows = nb * ls
    b_blk0 = D_INNER // D_STATE
    c_blk0 = b_blk0 + N_GROUPS
    return pl.pallas_call(
        functools.partial(_ssd_step_kernel, nb=nb, ls=ls),
        grid=(n_seq // nb, N_GROUPS),
        in_specs=[
            pl.BlockSpec((rows, GROUP_DIM), lambda i, g: (i, g)),
            pl.BlockSpec((rows, D_STATE), lambda i, g: (i, b_blk0 + g)),
            pl.BlockSpec((rows, D_STATE), lambda i, g: (i, c_blk0 + g)),
            pl.BlockSpec((rows, LANES), lambda i, g: (i, 0)),
            pl.BlockSpec((nb, 1, GROUP_DIM, D_STATE), lambda i, g: (i, g, 0, 0)),
            _const_spec((1, LANES)),
            _const_spec((1, LANES)),
        ],
        out_specs=[
            pl.BlockSpec((rows, GROUP_DIM), lambda i, g: (i, g)),
            pl.BlockSpec((nb, 1, GROUP_DIM, D_STATE), lambda i, g: (i, g, 0, 0)),
        ],
        out_shape=[
            jax.ShapeDtypeStruct((t, D_INNER), F32),
            jax.ShapeDtypeStruct((n_seq, N_GROUPS, GROUP_DIM, D_STATE), F32),
        ],
        compiler_params=_params(2),
        name="ssd_step",
    )(xbc, xbc, xbc, dt, st, alog, dsk)


def _ssd_post_kernel(y_ref, z_ref, h_ref, ng_ref, wo_ref, o_ref):
    acc = h_ref[...]
    for g in range(N_GROUPS):
        lo, hi = g * GROUP_DIM, (g + 1) * GROUP_DIM
        v = y_ref[:, lo:hi] * _silu(z_ref[:, lo:hi])
        vn = v * lax.rsqrt(jnp.mean(v * v, axis=-1, keepdims=True) + EPS) * ng_ref[:, lo:hi]
        acc = acc + _dot(vn.astype(BF16), wo_ref[lo:hi, :])
    o_ref[...] = acc


def _ssd_post(y, z, h, ng, wo, *, tm):
    t = y.shape[0]
    return pl.pallas_call(
        _ssd_post_kernel,
        grid=(t // tm,),
        in_specs=[
            pl.BlockSpec((tm, D_INNER), lambda i: (i, 0)),
            pl.BlockSpec((tm, D_INNER), lambda i: (i, 0)),
            pl.BlockSpec((tm, D_MODEL), lambda i: (i, 0)),
            _const_spec((1, D_INNER)),
            _const_spec((D_INNER, D_MODEL)),
        ],
        out_specs=pl.BlockSpec((tm, D_MODEL), lambda i: (i, 0)),
        out_shape=jax.ShapeDtypeStruct((t, D_MODEL), F32),
        compiler_params=_params(1),
        name="ssd_post",
    )(y, z, h, ng, wo)


def _trunk(h, cc_hist, sc_hist, ssm, w, *, tb, tl, tm, short, tl_front=None):
    bsz, seq, _ = h.shape
    t = bsz * seq
    if tb == 1 and tl % 64 == 0:
        h2, new_cc = _layer0(h, cc_hist, w, tl=tl)
        h2 = h2.reshape(t, D_MODEL)
    else:
        h1, new_cc = _conformer(h, cc_hist, w["gm0"], w["w_pw1"], w["b_pw1"], w["w_dw"], w["b_dw"], w["ln_g"],
                                w["ln_b"], w["w_pw2"], w["b_pw2"], tb=tb, tl=tl)
        h2 = _ffn(h1.reshape(t, D_MODEL), w["gf0"], w["wg0"], w["wu0"], w["wd0"], w["g_final"], tm=tm, final=False)
    z, xbc, dt, new_sc = _ssd_front(h2.reshape(bsz, seq, D_MODEL), sc_hist, w["gm1"], w["w_in"], w["w_dt"],
                                    w["conv_w"], w["conv_b"], w["dt_bias"], tb=tb, tl=tl_front or tl)
    if short:
        y, new_ss = _ssd_step(xbc.reshape(t, CONV_DIM), dt.reshape(t, LANES), ssm, w["a_log"], w["d_skip"],
                              nb=LANES // seq, ls=seq)
        h3 = _ssd_post(y, z.reshape(t, D_INNER), h2, w["norm_g"], w["w_out"], tm=tm)
    else:
        pad = (-seq) % LANES
        rows = lambda v: jnp.pad(v, ((0, 0), (0, pad), (0, 0))) if pad else v
        h3, new_ss = _ssd_scan(rows(xbc), rows(dt), rows(z), rows(h2.reshape(bsz, seq, D_MODEL)), ssm,
                               w["a_log"], w["d_skip_lanes"], w["norm_g"], w["w_out"], lc=LANES)
        h3 = h3[:, :seq].reshape(t, D_MODEL)
    out = _ffn(h3, w["gf1"], w["wg1"], w["wu1"], w["wd1"], w["g_final"], tm=tm, final=True)
    return out.reshape(bsz, seq, D_MODEL), new_cc, new_sc, new_ss


def _pad_lanes(v):
    return jnp.pad(v.astype(F32), (0, LANES - v.shape[0])).reshape(1, LANES)


def kernel(x_prompt, x_sample, cache_conv, state_ssd_conv, state_ssm, meta_tokens, norm_mix, norm_ffn, norm_final, cf_w_pw1, cf_b_pw1, cf_w_dw, cf_b_dw, cf_ln_g, cf_ln_b, cf_w_pw2, cf_b_pw2, ssd_w_in, ssd_conv_w, ssd_conv_b, ssd_dt_bias, ssd_a_log, ssd_d, ssd_norm_g, ssd_w_out, ffn_w_gate, ffn_w_up, ffn_w_down):
    row = lambda v: v.astype(F32).reshape(1, -1)
    w_in = ssd_w_in[0]
    w = {
        "gm0": row(norm_mix[0]), "gm1": row(norm_mix[1]),
        "gf0": row(norm_ffn[0]), "gf1": row(norm_ffn[1]), "g_final": row(norm_final),
        "w_pw1": cf_w_pw1[0].astype(BF16), "b_pw1": row(cf_b_pw1[0]),
        "w_dw": jnp.broadcast_to(cf_w_dw[0][:, None, :], (CONV_WIDTH, SUBLANES, D_MODEL)),
        "b_dw": row(cf_b_dw[0]), "ln_g": row(cf_ln_g[0]), "ln_b": row(cf_ln_b[0]),
        "w_pw2": cf_w_pw2[0].astype(BF16), "b_pw2": row(cf_b_pw2[0]),
        "w_in": w_in.astype(BF16),
        "w_dt": jnp.pad(w_in[:, D_INNER + CONV_DIM:], ((0, 0), (0, LANES - N_HEADS))).astype(BF16),
        "conv_w": jnp.broadcast_to(ssd_conv_w[0][:, None, :], (SSD_CONV_WIDTH, SUBLANES, CONV_DIM)),
        "conv_b": row(ssd_conv_b[0]),
        "dt_bias": _pad_lanes(ssd_dt_bias[0]), "a_log": _pad_lanes(ssd_a_log[0]), "d_skip": _pad_lanes(ssd_d[0]),
        "d_skip_lanes": jnp.repeat(ssd_d[0].astype(F32), HEAD_DIM).reshape(1, D_INNER),
        "norm_g": row(ssd_norm_g[0]), "w_out": ssd_w_out[0].astype(BF16),
        "wg0": ffn_w_gate[0].astype(BF16), "wu0": ffn_w_up[0].astype(BF16), "wd0": ffn_w_down[0].astype(BF16),
        "wg1": ffn_w_gate[1].astype(BF16), "wu1": ffn_w_up[1].astype(BF16), "wd1": ffn_w_down[1].astype(BF16),
    }
    bp, seq, _ = x_prompt.shape
    bs, dseq, _ = x_sample.shape

    _, cc_m, sc_m, ss_m = _trunk(
        meta_tokens.astype(F32)[None], jnp.zeros((1, CONV_HIST, D_MODEL), F32),
        jnp.zeros((1, SSD_CONV_HIST, CONV_DIM), F32), jnp.zeros((1, N_GROUPS, GROUP_DIM, D_STATE), F32), w,
        tb=1, tl=N_META, tm=N_META, short=False)

    y_prompt, cc_p, sc_p, ss_p = _trunk(
        x_prompt, cc_m, sc_m,
        ss_m, w, tb=1, tl=512, tl_front=1024, tm=1024, short=False)

    y_sample, cc_s, sc_s, ss_s = _trunk(
        x_sample, cache_conv[0], state_ssd_conv[0], state_ssm[0].reshape(bs, N_GROUPS, GROUP_DIM, D_STATE), w,
        tb=32, tl=dseq, tm=512, short=True)

    unpack_ss = lambda v: v.reshape(1, v.shape[0], N_HEADS, HEAD_DIM, D_STATE)
    return (y_prompt, y_sample, cc_p[None], cc_s[None], sc_p[None], sc_s[None], unpack_ss(ss_p), unpack_ss(ss_s))
```

```python
import functools

import jax
import jax.numpy as jnp
from jax import lax
from jax.experimental import pallas as pl
from jax.experimental.pallas import tpu as pltpu

D_MODEL = 1024
N_META = 16
CONV_WIDTH = 31
CONV_HIST = CONV_WIDTH - 1
D_INNER = 2048
HEAD_DIM = 64
N_HEADS = 32
N_GROUPS = 8
HEADS_PER_GROUP = 4
GROUP_DIM = HEADS_PER_GROUP * HEAD_DIM
D_STATE = 128
SSD_CONV_WIDTH = 4
SSD_CONV_HIST = SSD_CONV_WIDTH - 1
CONV_DIM = D_INNER + 2 * N_GROUPS * D_STATE
D_FF = 2816
EPS = 1e-6

LANES = 128
SUBLANES = 8
HIST_PAD = 32
SSD_HIST_PAD = 8
FRONT_COLS = 256
SCAN_CHUNKS = 4
FFN_COLS = 512
VMEM_LIMIT = 56 * 1024 * 1024

F32 = jnp.float32
BF16 = jnp.bfloat16
_NT = (((1,), (1,)), ((), ()))


def _const_spec(shape):
    return pl.BlockSpec(shape, lambda *_: (0,) * len(shape), pipeline_mode=pl.Buffered(1))


def _params(n_axes):
    return pltpu.CompilerParams(dimension_semantics=("arbitrary",) * n_axes, vmem_limit_bytes=VMEM_LIMIT)


def _rms(x, g):
    return x * lax.rsqrt(jnp.mean(x * x, axis=-1, keepdims=True) + EPS) * g


def _sigmoid(x):
    return 1.0 / (1.0 + jnp.exp(-x))


def _silu(x):
    return x * _sigmoid(x)


def _dot(a, b):
    return jnp.dot(a, b, preferred_element_type=F32)


def _roll_groups_down(x, d):
    b, n, c = x.shape
    return pltpu.roll(x.reshape(b * n // SUBLANES, SUBLANES, c), d, axis=1).reshape(b, n, c)


def _shift_rows_up(x, s):
    n = x.shape[0] - SUBLANES
    rot = [pltpu.roll(x[j:j + SUBLANES], SUBLANES - s, axis=0) for j in range(0, n + SUBLANES, SUBLANES)]
    keep = lax.broadcasted_iota(jnp.int32, (SUBLANES, x.shape[1]), 0) < SUBLANES - s
    return jnp.concatenate([jnp.where(keep, rot[j], rot[j + 1]) for j in range(n // SUBLANES)], axis=0)


def _dwconv(src_ref, w_ref, bias_ref, dst_ref, *, tb, tl, rc, cc, taps, base, act, cols=None, static=False):
    n_rc = tl // rc
    c_lo, c_hi = cols if cols is not None else (0, src_ref.shape[-1])
    by_shift = {}
    for k in range(taps):
        by_shift.setdefault((base + k) % SUBLANES, []).append(k)

    def piece(b, r0, c0):
        acc = jnp.broadcast_to(bias_ref[:, c0:c0 + cc], (rc, cc))
        for s, ks in sorted(by_shift.items()):
            n_rows = rc + SUBLANES if s else rc
            part = None
            for k in ks:
                q = (base + k) // SUBLANES
                blk = src_ref[b, pl.ds(r0 + SUBLANES * q, n_rows), c0:c0 + cc]
                wk = jnp.tile(w_ref[k, :, c0:c0 + cc], (n_rows // SUBLANES, 1))
                part = blk * wk if part is None else part + blk * wk
            acc = acc + (_shift_rows_up(part, s) if s else part)
        dst_ref[b, pl.ds(r0, rc), c0:c0 + cc] = act(acc)

    def chunk(b, r0):
        for c0 in range(c_lo, c_hi, cc):
            piece(b, r0, c0)

    if static == "deferred":
        return [functools.partial(piece, b, j * rc, c0)
                for b in range(tb) for j in range(n_rc) for c0 in range(c_lo, c_hi, cc)]
    if static:
        for b in range(tb):
            for j in range(n_rc):
                chunk(b, j * rc)
    else:
        def body(i, carry):
            chunk(i // n_rc, pl.multiple_of((i % n_rc) * rc, SUBLANES))
            return carry

        lax.fori_loop(0, tb * n_rc, body, 0)


def _conformer_kernel(h_ref, hist_ref, gm_ref, w1_ref, b1_ref, wdw_ref, bdw_ref, lng_ref, lnb_ref, w2_ref, b2_ref,
                      out_ref, nh_ref, gh_ref, c_ref, *, tb, tl, rc):
    l = pl.program_id(1)
    m = tb * tl

    @pl.when(l == 0)
    def _():
        gh_ref[:, 0:HIST_PAD - CONV_HIST, :] = jnp.zeros((tb, HIST_PAD - CONV_HIST, D_MODEL), F32)
        gh_ref[:, HIST_PAD - CONV_HIST:HIST_PAD, :] = hist_ref[...]

    x = h_ref[...].reshape(m, D_MODEL)
    hn = _rms(x, gm_ref[...]).astype(BF16)
    u = _dot(hn, w1_ref[...]) + b1_ref[...]
    g = u[:, :D_MODEL] * _sigmoid(u[:, D_MODEL:])
    gh_ref[:, HIST_PAD:HIST_PAD + tl, :] = g.reshape(tb, tl, D_MODEL)

    _dwconv(gh_ref, wdw_ref, bdw_ref, c_ref, tb=tb, tl=tl, rc=rc, cc=LANES if rc >= 64 else 4 * LANES,
            taps=CONV_WIDTH, base=HIST_PAD - CONV_HIST, act=lambda v: v)

    @pl.when(l == pl.num_programs(1) - 1)
    def _():
        nh_ref[...] = gh_ref[:, tl + HIST_PAD - CONV_HIST:tl + HIST_PAD, :]

    gh_ref[:, 0:HIST_PAD, :] = gh_ref[:, tl:tl + HIST_PAD, :]

    c = c_ref[...].reshape(m, D_MODEL)
    mu = jnp.mean(c, axis=-1, keepdims=True)
    xc = c - mu
    cn = xc * lax.rsqrt(jnp.mean(xc * xc, axis=-1, keepdims=True) + EPS) * lng_ref[...] + lnb_ref[...]
    cn = _silu(cn).astype(BF16)
    y = _dot(cn, w2_ref[...]) + b2_ref[...] + x
    out_ref[...] = y.reshape(tb, tl, D_MODEL)


def _conformer(h, hist, gm, w1, b1, wdw, bdw, lng, lnb, w2, b2, *, tb, tl):
    bsz, seq, _ = h.shape
    rc = 64 if tl % 64 == 0 else (16 if tl % 16 == 0 else SUBLANES)
    kern = functools.partial(_conformer_kernel, tb=tb, tl=tl, rc=rc)
    return pl.pallas_call(
        kern,
        grid=(bsz // tb, seq // tl),
        in_specs=[
            pl.BlockSpec((tb, tl, D_MODEL), lambda b, l: (b, l, 0)),
            pl.BlockSpec((tb, CONV_HIST, D_MODEL), lambda b, l: (b, 0, 0)),
            _const_spec((1, D_MODEL)),
            _const_spec((D_MODEL, 2 * D_MODEL)),
            _const_spec((1, 2 * D_MODEL)),
            _const_spec((CONV_WIDTH, SUBLANES, D_MODEL)),
            _const_spec((1, D_MODEL)),
            _const_spec((1, D_MODEL)),
            _const_spec((1, D_MODEL)),
            _const_spec((D_MODEL, D_MODEL)),
            _const_spec((1, D_MODEL)),
        ],
        out_specs=[
            pl.BlockSpec((tb, tl, D_MODEL), lambda b, l: (b, l, 0)),
            pl.BlockSpec((tb, CONV_HIST, D_MODEL), lambda b, l: (b, 0, 0)),
        ],
        out_shape=[
            jax.ShapeDtypeStruct((bsz, seq, D_MODEL), F32),
            jax.ShapeDtypeStruct((bsz, CONV_HIST, D_MODEL), F32),
        ],
        scratch_shapes=[
            pltpu.VMEM((tb, HIST_PAD + tl, D_MODEL), F32),
            pltpu.VMEM((tb, tl, D_MODEL), F32),
        ],
        compiler_params=_params(2),
        name="conformer",
    )(h, hist, gm, w1, b1, wdw, bdw, lng, lnb, w2, b2)


def _conformer_tm_kernel(x_ref, hist_ref, gm_ref, w1_ref, b1_ref, wdw_ref, bdw_ref, lng_ref, lnb_ref, w2_ref, b2_ref,
                         out_ref, nh_ref, gh_ref, c_ref, *, ls, tb, cc):
    m = ls * tb
    x = x_ref[...].reshape(m, D_MODEL)
    hn = _rms(x, gm_ref[...]).astype(BF16)
    u = _dot(hn, w1_ref[...]) + b1_ref[...]
    gh_ref[0:CONV_HIST] = hist_ref[...]
    gh_ref[CONV_HIST:CONV_HIST + ls] = (u[:, :D_MODEL] * _sigmoid(u[:, D_MODEL:])).reshape(ls, tb, D_MODEL)

    def conv_token(t, carry):
        for c0 in range(0, D_MODEL, cc):
            acc = jnp.broadcast_to(bdw_ref[:, c0:c0 + cc], (tb, cc))
            for k in range(CONV_WIDTH):
                acc = acc + gh_ref[t + k, :, c0:c0 + cc] * jnp.tile(wdw_ref[k, :, c0:c0 + cc], (tb // SUBLANES, 1))
            c_ref[t, :, c0:c0 + cc] = acc
        return carry

    lax.fori_loop(0, ls, conv_token, 0)
    nh_ref[...] = gh_ref[ls:ls + CONV_HIST]

    c = c_ref[...].reshape(m, D_MODEL)
    xc = c - jnp.mean(c, axis=-1, keepdims=True)
    cn = xc * lax.rsqrt(jnp.mean(xc * xc, axis=-1, keepdims=True) + EPS) * lng_ref[...] + lnb_ref[...]
    y = _dot(_silu(cn).astype(BF16), w2_ref[...]) + b2_ref[...] + x
    out_ref[...] = y.reshape(ls, tb, D_MODEL)


def _conformer_tm(x, hist, w, *, tb):
    ls, bsz, _ = x.shape
    seq_blk = lambda rows: pl.BlockSpec((rows, tb, D_MODEL), lambda i: (0, i, 0))
    return pl.pallas_call(
        functools.partial(_conformer_tm_kernel, ls=ls, tb=tb, cc=4 * LANES),
        grid=(bsz // tb,),
        in_specs=[
            seq_blk(ls),
            seq_blk(CONV_HIST),
            _const_spec((1, D_MODEL)),
            _const_spec((D_MODEL, 2 * D_MODEL)),
            _const_spec((1, 2 * D_MODEL)),
            _const_spec((CONV_WIDTH, SUBLANES, D_MODEL)),
            _const_spec((1, D_MODEL)),
            _const_spec((1, D_MODEL)),
            _const_spec((1, D_MODEL)),
            _const_spec((D_MODEL, D_MODEL)),
            _const_spec((1, D_MODEL)),
        ],
        out_specs=[seq_blk(ls), seq_blk(CONV_HIST)],
        out_shape=[
            jax.ShapeDtypeStruct((ls, bsz, D_MODEL), F32),
            jax.ShapeDtypeStruct((CONV_HIST, bsz, D_MODEL), F32),
        ],
        scratch_shapes=[
            pltpu.VMEM((CONV_HIST + ls, tb, D_MODEL), F32),
            pltpu.VMEM((ls, tb, D_MODEL), F32),
        ],
        compiler_params=_params(1),
        name="conformer_tm",
    )(x, hist, w["gm0"], w["w_pw1"], w["b_pw1"], w["w_dw"], w["b_dw"], w["ln_g"], w["ln_b"], w["w_pw2"], w["b_pw2"])


def _ffn_kernel(x_ref, g_ref, wg_ref, wu_ref, wd_ref, gf_ref, o_ref, *, final, cols):
    x = x_ref[...]
    hn = _rms(x, g_ref[...]).astype(BF16)
    y = x
    for lo in range(0, D_FF, cols):
        hi = min(lo + cols, D_FF)
        t = (_silu(_dot(hn, wg_ref[:, lo:hi])) * _dot(hn, wu_ref[:, lo:hi])).astype(BF16)
        y = y + _dot(t, wd_ref[lo:hi, :])
        if hi < D_FF:
            o_ref[...] = y
            y = o_ref[...]
    if final:
        y = _rms(y, gf_ref[...])
    o_ref[...] = y


def _ffn(x, g, wg, wu, wd, gf, *, tm, final):
    t = x.shape[0]
    cols = D_FF if tm <= 512 else FFN_COLS
    return pl.pallas_call(
        functools.partial(_ffn_kernel, final=final, cols=cols),
        grid=(t // tm,),
        in_specs=[
            pl.BlockSpec((tm, D_MODEL), lambda i: (i, 0)),
            _const_spec((1, D_MODEL)),
            _const_spec((D_MODEL, D_FF)),
            _const_spec((D_MODEL, D_FF)),
            _const_spec((D_FF, D_MODEL)),
            _const_spec((1, D_MODEL)),
        ],
        out_specs=pl.BlockSpec((tm, D_MODEL), lambda i: (i, 0)),
        out_shape=jax.ShapeDtypeStruct((t, D_MODEL), F32),
        compiler_params=_params(1),
        name="ffn_final" if final else "ffn",
    )(x, g, wg, wu, wd, gf)


def _layer0_kernel(h_ref, hist_ref, gm_ref, w1_ref, b1_ref, wdw_ref, bdw_ref, lng_ref, lnb_ref, w2_ref, b2_ref,
                   gf_ref, wg_ref, wu_ref, wd_ref, out_ref, nh_ref, gh_ref, c_ref, h1_ref, *, tl, rc, nl, nt):
    s = pl.program_id(0)
    slot = s % 2
    tile = jnp.minimum(s, nt - 1)

    @pl.when(s == 0)
    def _():
        h1_ref[1] = jnp.zeros((tl, D_MODEL), F32)

    @pl.when(tile % nl == 0)
    def _():
        gh_ref[:, 0:HIST_PAD - CONV_HIST, :] = jnp.zeros((1, HIST_PAD - CONV_HIST, D_MODEL), F32)
        gh_ref[:, HIST_PAD - CONV_HIST:HIST_PAD, :] = hist_ref[...]

    x1 = h1_ref[1 - slot]
    hn1 = _rms(x1, gf_ref[...]).astype(BF16)
    out_ref[0] = x1

    def ffn_block(lo, hi):
        t = (_silu(_dot(hn1, wg_ref[:, lo:hi])) * _dot(hn1, wu_ref[:, lo:hi])).astype(BF16)
        out_ref[0] = out_ref[0] + _dot(t, wd_ref[lo:hi, :])

    x = h_ref[0]
    hn = _rms(x, gm_ref[...]).astype(BF16)
    u = _dot(hn, w1_ref[...]) + b1_ref[...]
    gh_ref[0, HIST_PAD:HIST_PAD + tl, :] = u[:, :D_MODEL] * _sigmoid(u[:, D_MODEL:])
    pieces = _dwconv(gh_ref, wdw_ref, bdw_ref, c_ref, tb=1, tl=tl, rc=rc, cc=LANES, taps=CONV_WIDTH,
                     base=HIST_PAD - CONV_HIST, act=lambda v: v, static="deferred")
    blocks = [(lo, min(lo + FFN_COLS, D_FF)) for lo in range(0, D_FF, FFN_COLS)]
    per_block = -(-len(pieces) // len(blocks))
    @pl.when(s >= 0)
    def _():
        for j, (lo, hi) in enumerate(blocks):
            ffn_block(lo, hi)
            for p in pieces[j * per_block:(j + 1) * per_block]:
                p()
    nh_ref[...] = gh_ref[:, tl + HIST_PAD - CONV_HIST:tl + HIST_PAD, :]
    gh_ref[:, 0:HIST_PAD, :] = gh_ref[:, tl:tl + HIST_PAD, :]
    c = c_ref[0]
    xc = c - jnp.mean(c, axis=-1, keepdims=True)
    cn = xc * lax.rsqrt(jnp.mean(xc * xc, axis=-1, keepdims=True) + EPS) * lng_ref[...] + lnb_ref[...]
    h1_ref[slot] = _dot(_silu(cn).astype(BF16), w2_ref[...]) + b2_ref[...] + x


def _layer0(h, hist, w, *, tl):
    bsz, seq, _ = h.shape
    nl = seq // tl
    nt = bsz * nl
    cur = lambda s: jnp.minimum(s, nt - 1)
    prev = lambda s: jnp.maximum(s - 1, 0)
    return pl.pallas_call(
        functools.partial(_layer0_kernel, tl=tl, rc=64, nl=nl, nt=nt),
        grid=(nt + 1,),
        in_specs=[
            pl.BlockSpec((1, tl, D_MODEL), lambda s: (cur(s) // nl, cur(s) % nl, 0)),
            pl.BlockSpec((1, CONV_HIST, D_MODEL), lambda s: (0 if hist.shape[0] == 1 else cur(s) // nl, 0, 0)),
            _const_spec((1, D_MODEL)),
            _const_spec((D_MODEL, 2 * D_MODEL)),
            _const_spec((1, 2 * D_MODEL)),
            _const_spec((CONV_WIDTH, SUBLANES, D_MODEL)),
            _const_spec((1, D_MODEL)),
            _const_spec((1, D_MODEL)),
            _const_spec((1, D_MODEL)),
            _const_spec((D_MODEL, D_MODEL)),
            _const_spec((1, D_MODEL)),
            _const_spec((1, D_MODEL)),
            _const_spec((D_MODEL, D_FF)),
            _const_spec((D_MODEL, D_FF)),
            _const_spec((D_FF, D_MODEL)),
        ],
        out_specs=[
            pl.BlockSpec((1, tl, D_MODEL), lambda s: (prev(s) // nl, prev(s) % nl, 0)),
            pl.BlockSpec((1, CONV_HIST, D_MODEL), lambda s: (cur(s) // nl, 0, 0)),
        ],
        out_shape=[
            jax.ShapeDtypeStruct((bsz, seq, D_MODEL), F32),
            jax.ShapeDtypeStruct((bsz, CONV_HIST, D_MODEL), F32),
        ],
        scratch_shapes=[
            pltpu.VMEM((1, HIST_PAD + tl, D_MODEL), F32),
            pltpu.VMEM((1, tl, D_MODEL), F32),
            pltpu.VMEM((2, tl, D_MODEL), F32),
        ],
        compiler_params=_params(1),
        name="layer0",
    )(h, hist, w["gm0"], w["w_pw1"], w["b_pw1"], w["w_dw"], w["b_dw"], w["ln_g"], w["ln_b"], w["w_pw2"], w["b_pw2"],
      w["gf0"], w["wg0"], w["wu0"], w["wd0"])


def _softplus(x):
    return jnp.maximum(x, 0.0) + jnp.log(1.0 + jnp.exp(-jnp.abs(x)))


def _ssd_front_kernel(h_ref, ch_ref, gm_ref, wz_ref, wxa_ref, wxb_ref, wdt_ref, cw_ref, cb_ref, dtb_ref,
                      z_ref, xbc_ref, dt_ref, nc_ref, hist_ref, *, tb, tl):
    l = pl.program_id(1)
    m = tb * tl

    @pl.when(l == 0)
    def _():
        hist_ref[:, 0:SSD_HIST_PAD - SSD_CONV_HIST, :] = jnp.zeros((tb, SSD_HIST_PAD - SSD_CONV_HIST, CONV_DIM), F32)
        hist_ref[:, SSD_HIST_PAD - SSD_CONV_HIST:, :] = ch_ref[...]

    x = h_ref[...].reshape(m, D_MODEL)
    hn = _rms(x, gm_ref[...]).astype(BF16)
    dt_ref[...] = _softplus(_dot(hn, wdt_ref[...]) + dtb_ref[...]).reshape(tb, tl, LANES)
    row_in_group = lax.broadcasted_iota(jnp.int32, (tb, tl, FRONT_COLS), 1) % SUBLANES

    def conv_block(xn, lo):
        cs = slice(lo, lo + FRONT_COLS)
        xh = jnp.concatenate([hist_ref[:, :, cs], xn], axis=1)
        tap = lambda k: jnp.tile(cw_ref[k, :, cs], (tl // SUBLANES, 1))[None]
        acc = cb_ref[:, cs][None] + xn * tap(SSD_CONV_WIDTH - 1)
        for d in range(1, SSD_CONV_WIDTH):
            rot = _roll_groups_down(xh, d)
            acc = acc + jnp.where(row_in_group < d, rot[:, :tl], rot[:, SUBLANES:]) * tap(SSD_CONV_WIDTH - 1 - d)
        xbc_ref[:, :, cs] = _silu(acc).astype(xbc_ref.dtype)
        hist_ref[:, :, cs] = xn[:, tl - SSD_HIST_PAD:]

    def z_block(zn, lo):
        z_ref[:, :, lo:lo + FRONT_COLS] = zn.astype(z_ref.dtype)

    tasks = []
    z_cols = list(range(0, D_INNER, FRONT_COLS))
    for i, lo in enumerate(range(0, CONV_DIM, FRONT_COLS)):
        w_ref, off = (wxa_ref, lo) if lo < D_INNER else (wxb_ref, lo - D_INNER)
        tasks.append((w_ref, off, conv_block, lo))
        if i % 2 == 1:
            tasks.append((wz_ref, z_cols[i // 2], z_block, z_cols[i // 2]))
    pending = None
    for w_ref, off, finish, lo in tasks:
        res = _dot(hn, w_ref[:, off:off + FRONT_COLS]).reshape(tb, tl, FRONT_COLS)
        if pending is not None:
            pending[0](pending[1], pending[2])
        pending = (finish, res, lo)
    pending[0](pending[1], pending[2])

    @pl.when(l == pl.num_programs(1) - 1)
    def _():
        nc_ref[...] = hist_ref[:, SSD_HIST_PAD - SSD_CONV_HIST:, :]


def _ssd_front(h, ch, gm, w_in, wdt, cw, cb, dtb, *, tb, tl):
    bsz, seq, _ = h.shape
    act_dtype = BF16 if tl % (2 * SUBLANES) == 0 else F32
    w_blk = lambda j: pl.BlockSpec((D_MODEL, D_INNER), lambda *_: (0, j), pipeline_mode=pl.Buffered(1))
    return pl.pallas_call(
        functools.partial(_ssd_front_kernel, tb=tb, tl=tl),
        grid=(bsz // tb, seq // tl),
        in_specs=[
            pl.BlockSpec((tb, tl, D_MODEL), lambda b, l: (b, l, 0)),
            pl.BlockSpec((tb, SSD_CONV_HIST, CONV_DIM), lambda b, l: (0 if ch.shape[0] == 1 else b, 0, 0)),
            _const_spec((1, D_MODEL)),
            w_blk(0),
            w_blk(1),
            w_blk(2),
            _const_spec((D_MODEL, LANES)),
            _const_spec((SSD_CONV_WIDTH, SUBLANES, CONV_DIM)),
            _const_spec((1, CONV_DIM)),
            _const_spec((1, LANES)),
        ],
        out_specs=[
            pl.BlockSpec((tb, tl, D_INNER), lambda b, l: (b, l, 0)),
            pl.BlockSpec((tb, tl, CONV_DIM), lambda b, l: (b, l, 0)),
            pl.BlockSpec((tb, tl, LANES), lambda b, l: (b, l, 0)),
            pl.BlockSpec((tb, SSD_CONV_HIST, CONV_DIM), lambda b, l: (b, 0, 0)),
        ],
        out_shape=[
            jax.ShapeDtypeStruct((bsz, seq, D_INNER), act_dtype),
            jax.ShapeDtypeStruct((bsz, seq, CONV_DIM), act_dtype),
            jax.ShapeDtypeStruct((bsz, seq, LANES), F32),
            jax.ShapeDtypeStruct((bsz, SSD_CONV_HIST, CONV_DIM), F32),
        ],
        scratch_shapes=[pltpu.VMEM((tb, SSD_HIST_PAD, CONV_DIM), F32)],
        compiler_params=_params(2),
        name="ssd_front",
    )(h, ch, gm, w_in, w_in, w_in, wdt, cw, cb, dtb)


def _ssd_front_tm_kernel(x_ref, ch_ref, gm_ref, wz_ref, wxa_ref, wxb_ref, wdt_ref, cw_ref, cb_ref, dtb_ref,
                         z_ref, xbc_ref, dt_ref, nc_ref, *, ls, tb):
    m = ls * tb
    hn = _rms(x_ref[...].reshape(m, D_MODEL), gm_ref[...]).astype(BF16)
    dt_ref[...] = _softplus(_dot(hn, wdt_ref[...]) + dtb_ref[...]).reshape(ls, tb, LANES)
    z_ref[...] = _dot(hn, wz_ref[...]).reshape(ls, tb, D_INNER)
    for lo in range(0, CONV_DIM, FRONT_COLS):
        cs = slice(lo, lo + FRONT_COLS)
        w_ref, off = (wxa_ref, lo) if lo < D_INNER else (wxb_ref, lo - D_INNER)
        xn = _dot(hn, w_ref[:, off:off + FRONT_COLS]).reshape(ls, tb, FRONT_COLS)
        xh = jnp.concatenate([ch_ref[:, :, cs], xn], axis=0)
        acc = cb_ref[:, cs][None]
        for k in range(SSD_CONV_WIDTH):
            acc = acc + xh[k:k + ls] * jnp.tile(cw_ref[k, :, cs], (tb // SUBLANES, 1))[None]
        xbc_ref[:, :, cs] = _silu(acc)
        nc_ref[:, :, cs] = xh[ls:ls + SSD_CONV_HIST]


def _ssd_front_tm(x, ch, w, *, tb):
    ls, bsz, _ = x.shape
    w_in = w["w_in"]
    w_blk = lambda j: pl.BlockSpec((D_MODEL, D_INNER), lambda *_: (0, j), pipeline_mode=pl.Buffered(1))
    seq_blk = lambda rows, width: pl.BlockSpec((rows, tb, width), lambda i: (0, i, 0))
    return pl.pallas_call(
        functools.partial(_ssd_front_tm_kernel, ls=ls, tb=tb),
        grid=(bsz // tb,),
        in_specs=[
            seq_blk(ls, D_MODEL),
            seq_blk(SSD_CONV_HIST, CONV_DIM),
            _const_spec((1, D_MODEL)),
            w_blk(0),
            w_blk(1),
            w_blk(2),
            _const_spec((D_MODEL, LANES)),
            _const_spec((SSD_CONV_WIDTH, SUBLANES, CONV_DIM)),
            _const_spec((1, CONV_DIM)),
            _const_spec((1, LANES)),
        ],
        out_specs=[seq_blk(ls, D_INNER), seq_blk(ls, CONV_DIM), seq_blk(ls, LANES), seq_blk(SSD_CONV_HIST, CONV_DIM)],
        out_shape=[
            jax.ShapeDtypeStruct((ls, bsz, D_INNER), F32),
            jax.ShapeDtypeStruct((ls, bsz, CONV_DIM), F32),
            jax.ShapeDtypeStruct((ls, bsz, LANES), F32),
            jax.ShapeDtypeStruct((SSD_CONV_HIST, bsz, CONV_DIM), F32),
        ],
        compiler_params=_params(1),
        name="ssd_front_tm",
    )(x, ch, w["gm1"], w_in, w_in, w_in, w["w_dt"], w["conv_w"], w["conv_b"], w["dt_bias"])


def _cumsum_rows(a, tri):
    a_hi = a.astype(BF16)
    r1 = a - a_hi.astype(F32)
    a_mid = r1.astype(BF16)
    a_lo = (r1 - a_mid.astype(F32)).astype(BF16)
    return _dot(tri, a_hi) + _dot(tri, a_mid) + _dot(tri, a_lo)


def _head_rows(mat, g, width):
    return jnp.concatenate(
        [jnp.broadcast_to(mat[HEADS_PER_GROUP * g + r:HEADS_PER_GROUP * g + r + 1, :], (HEAD_DIM, width))
         for r in range(HEADS_PER_GROUP)], axis=0)


def _scan_fns(xbc_ref, z_ref, st_ref, vn_ref, alog_ref, dske_ref, ng_ref, lc):
    row = lax.broadcasted_iota(jnp.int32, (lc, lc), 0)
    col = lax.broadcasted_iota(jnp.int32, (lc, lc), 1)
    causal = row >= col
    tri = jnp.where(causal, 1.0, 0.0).astype(BF16)
    lane_head = lax.broadcasted_iota(jnp.int32, (lc, GROUP_DIM), 1) // HEAD_DIM
    lane_lo = lax.broadcasted_iota(jnp.int32, (lc, LANES), 1) < HEAD_DIM
    neg_a = -jnp.exp(alog_ref[...])

    def group_dots(i, r0, g):
        b_off = D_INNER + g * D_STATE
        c_off = D_INNER + N_GROUPS * D_STATE + g * D_STATE
        bg = xbc_ref[i, r0:r0 + lc, b_off:b_off + D_STATE].astype(BF16)
        cg = xbc_ref[i, r0:r0 + lc, c_off:c_off + D_STATE].astype(BF16)
        cb = lax.dot_general(cg, bg, _NT, preferred_element_type=F32)
        hg = st_ref[i, g]
        yoff = lax.dot_general(cg, hg.astype(BF16), _NT, preferred_element_type=F32)
        return bg, cb, hg, yoff

    def decays(dt):
        acs = _cumsum_rows(dt * neg_a, tri)
        acs_t = acs.T
        dt_t = dt.T
        last_t = jnp.broadcast_to(acs_t[:, lc - 1:lc], (LANES, lc))
        c_t = acs_t - jnp.log(dt_t)
        coef_t = jnp.exp(last_t - acs_t) * dt_t
        cd = jnp.broadcast_to(jnp.exp(acs_t[:, lc - 1:lc]), (LANES, D_STATE))
        return acs, c_t, coef_t, cd

    def group_tail(i, r0, g, dots, dec):
        bg, cb, hg, yoff = dots
        acs, c_t, coef_t, cd = dec
        lo, hi = g * GROUP_DIM, (g + 1) * GROUP_DIM
        xg = xbc_ref[i, r0:r0 + lc, lo:hi].astype(F32)
        ws, acs_b = [], []
        for r in range(HEADS_PER_GROUP):
            h = HEADS_PER_GROUP * g + r
            ab = jnp.broadcast_to(acs[:, h:h + 1], (lc, LANES))
            acs_b.append(ab)
            ws.append((cb * jnp.exp(jnp.where(causal, ab - c_t[h:h + 1, :], -jnp.inf))).astype(BF16))
        x_bd = jnp.concatenate([jnp.where(lane_head == r, xg, 0.0).astype(BF16) for r in range(HEADS_PER_GROUP)],
                               axis=0)
        yd = _dot(jnp.concatenate(ws, axis=1), x_bd)
        e_b = jnp.exp(jnp.concatenate([jnp.where(lane_lo, acs_b[0], acs_b[1]),
                                       jnp.where(lane_lo, acs_b[2], acs_b[3])], axis=1))
        y = yd + e_b * yoff + dske_ref[:, lo:hi] * xg

        xcoef_t = (xg.T * _head_rows(coef_t, g, lc)).astype(BF16)
        st_ref[i, g] = _head_rows(cd, g, D_STATE) * hg + _dot(xcoef_t, bg)

        v = y * _silu(z_ref[i, r0:r0 + lc, lo:hi].astype(F32))
        vn = v * lax.rsqrt(jnp.mean(v * v, axis=-1, keepdims=True) + EPS) * ng_ref[:, lo:hi]
        vn_ref[i, r0:r0 + lc, lo:hi] = vn.astype(BF16)

    return group_dots, decays, group_tail


def _ssd_scan_kernel(xbc_ref, dt_ref, dtn_ref, z_ref, h_ref, h0_ref, alog_ref, dske_ref, ng_ref, wo_ref, o_ref, st_ref,
                     vn_ref, dec_ref, *, lc, n_chunks):
    c = pl.program_id(1)
    first_slot = 0 if n_chunks % 2 == 0 else c % 2
    group_dots, decays, group_tail = _scan_fns(xbc_ref, z_ref, st_ref, vn_ref, alog_ref, dske_ref, ng_ref, lc)

    @pl.when(c == 0)
    def _():
        st_ref[...] = h0_ref[...]
        for k, v in enumerate(decays(dt_ref[0, 0:lc, :])):
            dec_ref[0, k] = v

    half = D_INNER // 2
    for ci in range(n_chunks):
        r0 = ci * lc
        slot = (first_slot + ci) % 2
        dots = group_dots(0, r0, 0)
        dec = tuple(dec_ref[slot, k] for k in range(4))
        for g in range(N_GROUPS):
            nxt = group_dots(0, r0, g + 1) if g + 1 < N_GROUPS else None
            group_tail(0, r0, g, dots, dec)
            dots = nxt
            if g == 1:
                next_dt = dt_ref[0, r0 + lc:r0 + 2 * lc, :] if ci + 1 < n_chunks else dtn_ref[0]
                for k, v in enumerate(decays(next_dt)):
                    dec_ref[1 - slot, k] = v
            if g == N_GROUPS // 2 - 1:
                o_ref[0, r0:r0 + lc, :] = h_ref[0, r0:r0 + lc, :] + _dot(vn_ref[0, r0:r0 + lc, :half], wo_ref[:half, :])
        o_ref[0, r0:r0 + lc, :] = o_ref[0, r0:r0 + lc, :] + _dot(vn_ref[0, r0:r0 + lc, half:], wo_ref[half:, :])


def _ssd_scan(xbc, dt, z, h, h0, alog, dske, ng, wo, *, lc):
    bsz, seq, _ = xbc.shape
    assert lc == LANES
    nb = 1
    n_chunks = SCAN_CHUNKS if seq % (SCAN_CHUNKS * lc) == 0 else 1
    rows = n_chunks * lc
    last = seq // lc - 1
    shared_h0 = h0.shape[0] == 1
    return pl.pallas_call(
        functools.partial(_ssd_scan_kernel, lc=lc, n_chunks=n_chunks),
        grid=(bsz // nb, seq // rows),
        in_specs=[
            pl.BlockSpec((nb, rows, CONV_DIM), lambda b, c: (b, c, 0)),
            pl.BlockSpec((nb, rows, LANES), lambda b, c: (b, c, 0)),
            pl.BlockSpec((nb, lc, LANES), lambda b, c: (b, jnp.minimum((c + 1) * n_chunks, last), 0)),
            pl.BlockSpec((nb, rows, D_INNER), lambda b, c: (b, c, 0)),
            pl.BlockSpec((nb, rows, D_MODEL), lambda b, c: (b, c, 0)),
            pl.BlockSpec((nb, N_GROUPS, GROUP_DIM, D_STATE), lambda b, c: (0 if shared_h0 else b, 0, 0, 0)),
            _const_spec((1, LANES)),
            _const_spec((1, D_INNER)),
            _const_spec((1, D_INNER)),
            _const_spec((D_INNER, D_MODEL)),
        ],
        out_specs=[
            pl.BlockSpec((nb, rows, D_MODEL), lambda b, c: (b, c, 0)),
            pl.BlockSpec((nb, N_GROUPS, GROUP_DIM, D_STATE), lambda b, c: (b, 0, 0, 0)),
        ],
        out_shape=[
            jax.ShapeDtypeStruct((bsz, seq, D_MODEL), F32),
            jax.ShapeDtypeStruct((bsz, N_GROUPS, GROUP_DIM, D_STATE), F32),
        ],
        scratch_shapes=[pltpu.VMEM((nb, rows, D_INNER), BF16), pltpu.VMEM((2, 4, LANES, LANES), F32)],
        compiler_params=_params(2),
        name="ssd_scan",
    )(xbc, dt, dt, z, h, h0, alog, dske, ng, wo)


def _ssd_step_kernel(x_ref, b_ref, c_ref, dt_ref, st_ref, alog_ref, dsk_ref, y_ref, so_ref, *, nb, ls):
    g = pl.program_id(1)
    rows = nb * ls
    shift = (LANES - HEADS_PER_GROUP * g) % LANES
    dt = pltpu.roll(dt_ref[...].reshape(rows, LANES), shift, axis=1)
    alog = pltpu.roll(alog_ref[...], shift, axis=1)
    dsk = pltpu.roll(dsk_ref[...], shift, axis=1)
    a = dt * (-jnp.exp(alog))

    row = lax.broadcasted_iota(jnp.int32, (rows, rows), 0)
    col = lax.broadcasted_iota(jnp.int32, (rows, rows), 1)
    same = (row % nb) == (col % nb)
    causal = jnp.logical_and(same, row >= col)
    hi = lax.Precision.HIGHEST
    acs = jnp.dot(jnp.where(causal, 1.0, 0.0), a, preferred_element_type=F32, precision=hi)
    tot = jnp.dot(jnp.where(same, 1.0, 0.0), a, preferred_element_type=F32, precision=hi)
    acs_t = acs.T
    coef = jnp.exp(tot - acs) * dt
    eacs = jnp.exp(acs)
    etot = jnp.exp(tot)

    bg = b_ref[...].reshape(rows, D_STATE).astype(BF16)
    cg = c_ref[...].reshape(rows, D_STATE).astype(BF16)
    xg = x_ref[...].reshape(rows, GROUP_DIM)
    cb = lax.dot_general(cg, bg, _NT, preferred_element_type=F32)

    seq_of_col = lax.broadcasted_iota(jnp.int32, (GROUP_DIM, rows), 1) % nb
    yoff_t = jnp.zeros((GROUP_DIM, rows), F32)
    for b in range(nb):
        yb = lax.dot_general(st_ref[b, 0].astype(BF16), cg, _NT, preferred_element_type=F32)
        yoff_t = jnp.where(seq_of_col == b, yb, yoff_t)
    yoff = yoff_t.T

    ys, xcs = [], []
    for r in range(HEADS_PER_GROUP):
        decay = jnp.exp(jnp.where(causal, acs[:, r:r + 1] - acs_t[r:r + 1, :], -jnp.inf))
        w = (cb * decay).astype(BF16)
        xr = xg[:, r * HEAD_DIM:(r + 1) * HEAD_DIM]
        yd = _dot(w, (xr * dt[:, r:r + 1]).astype(BF16))
        ys.append(yd + eacs[:, r:r + 1] * yoff[:, r * HEAD_DIM:(r + 1) * HEAD_DIM] + dsk[:, r:r + 1] * xr)
        xcs.append(xr * coef[:, r:r + 1])
    y_ref[...] = jnp.concatenate(ys, axis=1).reshape(ls, nb, GROUP_DIM)
    xcoef_t = jnp.concatenate(xcs, axis=1).T

    for b in range(nb):
        upd = _dot(jnp.where(seq_of_col == b, xcoef_t, 0.0).astype(BF16), bg)
        scale = jnp.concatenate(
            [jnp.broadcast_to(etot[b:b + 1, r:r + 1], (HEAD_DIM, D_STATE)) for r in range(HEADS_PER_GROUP)],
            axis=0)
        so_ref[b, 0] = scale * st_ref[b, 0] + upd


def _ssd_step(xbc, dt, st, alog, dsk, *, nb, ls):
    n_seq = st.shape[0]
    b_blk0 = D_INNER // D_STATE
    c_blk0 = b_blk0 + N_GROUPS
    return pl.pallas_call(
        functools.partial(_ssd_step_kernel, nb=nb, ls=ls),
        grid=(n_seq // nb, N_GROUPS),
        in_specs=[
            pl.BlockSpec((ls, nb, GROUP_DIM), lambda i, g: (0, i, g)),
            pl.BlockSpec((ls, nb, D_STATE), lambda i, g: (0, i, b_blk0 + g)),
            pl.BlockSpec((ls, nb, D_STATE), lambda i, g: (0, i, c_blk0 + g)),
            pl.BlockSpec((ls, nb, LANES), lambda i, g: (0, i, 0)),
            pl.BlockSpec((nb, 1, GROUP_DIM, D_STATE), lambda i, g: (i, g, 0, 0)),
            _const_spec((1, LANES)),
            _const_spec((1, LANES)),
        ],
        out_specs=[
            pl.BlockSpec((ls, nb, GROUP_DIM), lambda i, g: (0, i, g)),
            pl.BlockSpec((nb, 1, GROUP_DIM, D_STATE), lambda i, g: (i, g, 0, 0)),
        ],
        out_shape=[
            jax.ShapeDtypeStruct((ls, n_seq, D_INNER), F32),
            jax.ShapeDtypeStruct((n_seq, N_GROUPS, GROUP_DIM, D_STATE), F32),
        ],
        compiler_params=_params(2),
        name="ssd_step",
    )(xbc, xbc, xbc, dt, st, alog, dsk)


def _ssd_post_kernel(y_ref, z_ref, h_ref, ng_ref, wo_ref, o_ref):
    acc = h_ref[...]
    for g in range(N_GROUPS):
        lo, hi = g * GROUP_DIM, (g + 1) * GROUP_DIM
        v = y_ref[:, lo:hi] * _silu(z_ref[:, lo:hi])
        vn = v * lax.rsqrt(jnp.mean(v * v, axis=-1, keepdims=True) + EPS) * ng_ref[:, lo:hi]
        acc = acc + _dot(vn.astype(BF16), wo_ref[lo:hi, :])
    o_ref[...] = acc


def _ssd_post(y, z, h, ng, wo, *, tm):
    t = y.shape[0]
    return pl.pallas_call(
        _ssd_post_kernel,
        grid=(t // tm,),
        in_specs=[
            pl.BlockSpec((tm, D_INNER), lambda i: (i, 0)),
            pl.BlockSpec((tm, D_INNER), lambda i: (i, 0)),
            pl.BlockSpec((tm, D_MODEL), lambda i: (i, 0)),
            _const_spec((1, D_INNER)),
            _const_spec((D_INNER, D_MODEL)),
        ],
        out_specs=pl.BlockSpec((tm, D_MODEL), lambda i: (i, 0)),
        out_shape=jax.ShapeDtypeStruct((t, D_MODEL), F32),
        compiler_params=_params(1),
        name="ssd_post",
    )(y, z, h, ng, wo)


def _trunk(h, cc_hist, sc_hist, ssm, w, *, tb, tl, tm, tl_front=None):
    bsz, seq, _ = h.shape
    t = bsz * seq
    if tb == 1 and tl % 64 == 0:
        h2, new_cc = _layer0(h, cc_hist, w, tl=tl)
        h2 = h2.reshape(t, D_MODEL)
    else:
        h1, new_cc = _conformer(h, cc_hist, w["gm0"], w["w_pw1"], w["b_pw1"], w["w_dw"], w["b_dw"], w["ln_g"],
                                w["ln_b"], w["w_pw2"], w["b_pw2"], tb=tb, tl=tl)
        h2 = _ffn(h1.reshape(t, D_MODEL), w["gf0"], w["wg0"], w["wu0"], w["wd0"], w["g_final"], tm=tm, final=False)
    z, xbc, dt, new_sc = _ssd_front(h2.reshape(bsz, seq, D_MODEL), sc_hist, w["gm1"], w["w_in"], w["w_dt"],
                                    w["conv_w"], w["conv_b"], w["dt_bias"], tb=tb, tl=tl_front or tl)
    pad = (-seq) % LANES
    rows = lambda v: jnp.pad(v, ((0, 0), (0, pad), (0, 0))) if pad else v
    h3, new_ss = _ssd_scan(rows(xbc), rows(dt), rows(z), rows(h2.reshape(bsz, seq, D_MODEL)), ssm,
                           w["a_log"], w["d_skip_lanes"], w["norm_g"], w["w_out"], lc=LANES)
    h3 = h3[:, :seq].reshape(t, D_MODEL)
    out = _ffn(h3, w["gf1"], w["wg1"], w["wu1"], w["wd1"], w["g_final"], tm=tm, final=True)
    return out.reshape(bsz, seq, D_MODEL), new_cc, new_sc, new_ss


def _trunk_short(x, cc_hist, sc_hist, ssm, w, *, tb, tm):
    ls, bsz, _ = x.shape
    t = ls * bsz
    h1, new_cc = _conformer_tm(x, cc_hist, w, tb=tb)
    h2 = _ffn(h1.reshape(t, D_MODEL), w["gf0"], w["wg0"], w["wu0"], w["wd0"], w["g_final"], tm=tm, final=False)
    z, xbc, dt, new_sc = _ssd_front_tm(h2.reshape(ls, bsz, D_MODEL), sc_hist, w, tb=tb)
    y, new_ss = _ssd_step(xbc, dt, ssm, w["a_log"], w["d_skip"], nb=LANES // ls, ls=ls)
    h3 = _ssd_post(y.reshape(t, D_INNER), z.reshape(t, D_INNER), h2, w["norm_g"], w["w_out"], tm=tm)
    out = _ffn(h3, w["gf1"], w["wg1"], w["wu1"], w["wd1"], w["g_final"], tm=tm, final=True)
    return out.reshape(ls, bsz, D_MODEL), new_cc, new_sc, new_ss


def _pad_lanes(v):
    return jnp.pad(v.astype(F32), (0, LANES - v.shape[0])).reshape(1, LANES)


def kernel(x_prompt, x_sample, cache_conv, state_ssd_conv, state_ssm, meta_tokens, norm_mix, norm_ffn, norm_final, cf_w_pw1, cf_b_pw1, cf_w_dw, cf_b_dw, cf_ln_g, cf_ln_b, cf_w_pw2, cf_b_pw2, ssd_w_in, ssd_conv_w, ssd_conv_b, ssd_dt_bias, ssd_a_log, ssd_d, ssd_norm_g, ssd_w_out, ffn_w_gate, ffn_w_up, ffn_w_down):
    row = lambda v: v.astype(F32).reshape(1, -1)
    w_in = ssd_w_in[0]
    w = {
        "gm0": row(norm_mix[0]), "gm1": row(norm_mix[1]),
        "gf0": row(norm_ffn[0]), "gf1": row(norm_ffn[1]), "g_final": row(norm_final),
        "w_pw1": cf_w_pw1[0].astype(BF16), "b_pw1": row(cf_b_pw1[0]),
        "w_dw": jnp.broadcast_to(cf_w_dw[0][:, None, :], (CONV_WIDTH, SUBLANES, D_MODEL)),
        "b_dw": row(cf_b_dw[0]), "ln_g": row(cf_ln_g[0]), "ln_b": row(cf_ln_b[0]),
        "w_pw2": cf_w_pw2[0].astype(BF16), "b_pw2": row(cf_b_pw2[0]),
        "w_in": w_in.astype(BF16),
        "w_dt": jnp.pad(w_in[:, D_INNER + CONV_DIM:], ((0, 0), (0, LANES - N_HEADS))).astype(BF16),
        "conv_w": jnp.broadcast_to(ssd_conv_w[0][:, None, :], (SSD_CONV_WIDTH, SUBLANES, CONV_DIM)),
        "conv_b": row(ssd_conv_b[0]),
        "dt_bias": _pad_lanes(ssd_dt_bias[0]), "a_log": _pad_lanes(ssd_a_log[0]), "d_skip": _pad_lanes(ssd_d[0]),
        "d_skip_lanes": jnp.repeat(ssd_d[0].astype(F32), HEAD_DIM).reshape(1, D_INNER),
        "norm_g": row(ssd_norm_g[0]), "w_out": ssd_w_out[0].astype(BF16),
        "wg0": ffn_w_gate[0].astype(BF16), "wu0": ffn_w_up[0].astype(BF16), "wd0": ffn_w_down[0].astype(BF16),
        "wg1": ffn_w_gate[1].astype(BF16), "wu1": ffn_w_up[1].astype(BF16), "wd1": ffn_w_down[1].astype(BF16),
    }
    bp, seq, _ = x_prompt.shape
    bs = x_sample.shape[0]

    _, cc_m, sc_m, ss_m = _trunk(
        meta_tokens.astype(F32)[None], jnp.zeros((1, CONV_HIST, D_MODEL), F32),
        jnp.zeros((1, SSD_CONV_HIST, CONV_DIM), F32), jnp.zeros((1, N_GROUPS, GROUP_DIM, D_STATE), F32), w,
        tb=1, tl=N_META, tm=N_META)

    y_prompt, cc_p, sc_p, ss_p = _trunk(
        x_prompt, cc_m, sc_m,
        ss_m, w, tb=1, tl=256, tl_front=1024, tm=512)

    swap = lambda v: jnp.transpose(v, (1, 0, 2))
    y_s, cc_s, sc_s, ss_s = _trunk_short(
        swap(x_sample), swap(cache_conv[0]), swap(state_ssd_conv[0]),
        state_ssm[0].reshape(bs, N_GROUPS, GROUP_DIM, D_STATE), w, tb=32, tm=512)

    unpack_ss = lambda v: v.reshape(1, v.shape[0], N_HEADS, HEAD_DIM, D_STATE)
    return (y_prompt, swap(y_s), cc_p[None], swap(cc_s)[None], sc_p[None], swap(sc_s)[None],
            unpack_ss(ss_p), unpack_ss(ss_s))
```

```python
import functools

import jax
import jax.numpy as jnp
from jax import lax
from jax.experimental import pallas as pl
from jax.experimental.pallas import tpu as pltpu

D_MODEL = 1024
N_META = 16
CONV_WIDTH = 31
CONV_HIST = CONV_WIDTH - 1
D_INNER = 2048
HEAD_DIM = 64
N_HEADS = 32
N_GROUPS = 8
HEADS_PER_GROUP = 4
GROUP_DIM = HEADS_PER_GROUP * HEAD_DIM
D_STATE = 128
SSD_CONV_WIDTH = 4
SSD_CONV_HIST = SSD_CONV_WIDTH - 1
CONV_DIM = D_INNER + 2 * N_GROUPS * D_STATE
D_FF = 2816
EPS = 1e-6

LANES = 128
SUBLANES = 8
HIST_PAD = 32
SSD_HIST_PAD = 8
FRONT_COLS = 256
SCAN_CHUNKS = 4
FFN_COLS = 512
CONV_PIECE_ROWS = 64
LAYER0_ROWS = 256
FRONT_ROWS = 1024
FFN_ROWS = 512
SHORT_SEQS = 32
V7X_VMEM_BYTES = 64 * 1024 * 1024
VMEM_LIMIT = V7X_VMEM_BYTES - 8 * 1024 * 1024

F32 = jnp.float32
BF16 = jnp.bfloat16
_NT = (((1,), (1,)), ((), ()))


def _const_spec(shape):
    return pl.BlockSpec(shape, lambda *_: (0,) * len(shape), pipeline_mode=pl.Buffered(1))


def _params(n_axes):
    return pltpu.CompilerParams(dimension_semantics=("arbitrary",) * n_axes, vmem_limit_bytes=VMEM_LIMIT)


def _rms(x, g):
    return x * lax.rsqrt(jnp.mean(x * x, axis=-1, keepdims=True) + EPS) * g


def _sigmoid(x):
    return 1.0 / (1.0 + jnp.exp(-x))


def _silu(x):
    return x * _sigmoid(x)


def _dot(a, b):
    return jnp.dot(a, b, preferred_element_type=F32)


def _roll_groups_down(x, d):
    b, n, c = x.shape
    return pltpu.roll(x.reshape(b * n // SUBLANES, SUBLANES, c), d, axis=1).reshape(b, n, c)


def _shift_rows_up(x, s):
    n = x.shape[0] - SUBLANES
    rot = [pltpu.roll(x[j:j + SUBLANES], SUBLANES - s, axis=0) for j in range(0, n + SUBLANES, SUBLANES)]
    keep = lax.broadcasted_iota(jnp.int32, (SUBLANES, x.shape[1]), 0) < SUBLANES - s
    return jnp.concatenate([jnp.where(keep, rot[j], rot[j + 1]) for j in range(n // SUBLANES)], axis=0)


def _dwconv(src_ref, w_ref, bias_ref, dst_ref, *, tb, tl, rc, cc, taps, base, act, cols=None, static=False):
    n_rc = tl // rc
    c_lo, c_hi = cols if cols is not None else (0, src_ref.shape[-1])
    by_shift = {}
    for k in range(taps):
        by_shift.setdefault((base + k) % SUBLANES, []).append(k)

    def piece(b, r0, c0):
        acc = jnp.broadcast_to(bias_ref[:, c0:c0 + cc], (rc, cc))
        for s, ks in sorted(by_shift.items()):
            n_rows = rc + SUBLANES if s else rc
            part = None
            for k in ks:
                q = (base + k) // SUBLANES
                blk = src_ref[b, pl.ds(r0 + SUBLANES * q, n_rows), c0:c0 + cc]
                wk = jnp.tile(w_ref[k, :, c0:c0 + cc], (n_rows // SUBLANES, 1))
                part = blk * wk if part is None else part + blk * wk
            acc = acc + (_shift_rows_up(part, s) if s else part)
        dst_ref[b, pl.ds(r0, rc), c0:c0 + cc] = act(acc)

    def chunk(b, r0):
        for c0 in range(c_lo, c_hi, cc):
            piece(b, r0, c0)

    if static == "deferred":
        return [functools.partial(piece, b, j * rc, c0)
                for b in range(tb) for j in range(n_rc) for c0 in range(c_lo, c_hi, cc)]
    if static:
        for b in range(tb):
            for j in range(n_rc):
                chunk(b, j * rc)
    else:
        def body(i, carry):
            chunk(i // n_rc, pl.multiple_of((i % n_rc) * rc, SUBLANES))
            return carry

        lax.fori_loop(0, tb * n_rc, body, 0)


def _conformer_kernel(h_ref, hist_ref, gm_ref, w1_ref, b1_ref, wdw_ref, bdw_ref, lng_ref, lnb_ref, w2_ref, b2_ref,
                      out_ref, nh_ref, gh_ref, c_ref, *, tb, tl, rc):
    l = pl.program_id(1)
    m = tb * tl

    @pl.when(l == 0)
    def _():
        gh_ref[:, 0:HIST_PAD - CONV_HIST, :] = jnp.zeros((tb, HIST_PAD - CONV_HIST, D_MODEL), F32)
        gh_ref[:, HIST_PAD - CONV_HIST:HIST_PAD, :] = hist_ref[...]

    x = h_ref[...].reshape(m, D_MODEL)
    hn = _rms(x, gm_ref[...]).astype(BF16)
    u = _dot(hn, w1_ref[...]) + b1_ref[...]
    g = u[:, :D_MODEL] * _sigmoid(u[:, D_MODEL:])
    gh_ref[:, HIST_PAD:HIST_PAD + tl, :] = g.reshape(tb, tl, D_MODEL)

    _dwconv(gh_ref, wdw_ref, bdw_ref, c_ref, tb=tb, tl=tl, rc=rc, cc=LANES if rc >= 64 else 4 * LANES,
            taps=CONV_WIDTH, base=HIST_PAD - CONV_HIST, act=lambda v: v)

    @pl.when(l == pl.num_programs(1) - 1)
    def _():
        nh_ref[...] = gh_ref[:, tl + HIST_PAD - CONV_HIST:tl + HIST_PAD, :]

    gh_ref[:, 0:HIST_PAD, :] = gh_ref[:, tl:tl + HIST_PAD, :]

    c = c_ref[...].reshape(m, D_MODEL)
    mu = jnp.mean(c, axis=-1, keepdims=True)
    xc = c - mu
    cn = xc * lax.rsqrt(jnp.mean(xc * xc, axis=-1, keepdims=True) + EPS) * lng_ref[...] + lnb_ref[...]
    cn = _silu(cn).astype(BF16)
    y = _dot(cn, w2_ref[...]) + b2_ref[...] + x
    out_ref[...] = y.reshape(tb, tl, D_MODEL)


def _conformer(h, hist, gm, w1, b1, wdw, bdw, lng, lnb, w2, b2, *, tb, tl):
    bsz, seq, _ = h.shape
    rc = 64 if tl % 64 == 0 else (16 if tl % 16 == 0 else SUBLANES)
    kern = functools.partial(_conformer_kernel, tb=tb, tl=tl, rc=rc)
    return pl.pallas_call(
        kern,
        grid=(bsz // tb, seq // tl),
        in_specs=[
            pl.BlockSpec((tb, tl, D_MODEL), lambda b, l: (b, l, 0)),
            pl.BlockSpec((tb, CONV_HIST, D_MODEL), lambda b, l: (b, 0, 0)),
            _const_spec((1, D_MODEL)),
            _const_spec((D_MODEL, 2 * D_MODEL)),
            _const_spec((1, 2 * D_MODEL)),
            _const_spec((CONV_WIDTH, SUBLANES, D_MODEL)),
            _const_spec((1, D_MODEL)),
            _const_spec((1, D_MODEL)),
            _const_spec((1, D_MODEL)),
            _const_spec((D_MODEL, D_MODEL)),
            _const_spec((1, D_MODEL)),
        ],
        out_specs=[
            pl.BlockSpec((tb, tl, D_MODEL), lambda b, l: (b, l, 0)),
            pl.BlockSpec((tb, CONV_HIST, D_MODEL), lambda b, l: (b, 0, 0)),
        ],
        out_shape=[
            jax.ShapeDtypeStruct((bsz, seq, D_MODEL), F32),
            jax.ShapeDtypeStruct((bsz, CONV_HIST, D_MODEL), F32),
        ],
        scratch_shapes=[
            pltpu.VMEM((tb, HIST_PAD + tl, D_MODEL), F32),
            pltpu.VMEM((tb, tl, D_MODEL), F32),
        ],
        compiler_params=_params(2),
        name="conformer",
    )(h, hist, gm, w1, b1, wdw, bdw, lng, lnb, w2, b2)


def _conformer_tm_kernel(x_ref, hist_ref, gm_ref, w1_ref, b1_ref, wdw_ref, bdw_ref, lng_ref, lnb_ref, w2_ref, b2_ref,
                         out_ref, nh_ref, gh_ref, c_ref, *, ls, tb, cc):
    m = ls * tb
    x = x_ref[...].reshape(m, D_MODEL)
    hn = _rms(x, gm_ref[...]).astype(BF16)
    u = _dot(hn, w1_ref[...]) + b1_ref[...]
    gh_ref[0:CONV_HIST] = hist_ref[...]
    gh_ref[CONV_HIST:CONV_HIST + ls] = (u[:, :D_MODEL] * _sigmoid(u[:, D_MODEL:])).reshape(ls, tb, D_MODEL)

    def conv_token(t, carry):
        for c0 in range(0, D_MODEL, cc):
            acc = jnp.broadcast_to(bdw_ref[:, c0:c0 + cc], (tb, cc))
            for k in range(CONV_WIDTH):
                acc = acc + gh_ref[t + k, :, c0:c0 + cc] * jnp.tile(wdw_ref[k, :, c0:c0 + cc], (tb // SUBLANES, 1))
            c_ref[t, :, c0:c0 + cc] = acc
        return carry

    lax.fori_loop(0, ls, conv_token, 0)
    nh_ref[...] = gh_ref[ls:ls + CONV_HIST]

    c = c_ref[...].reshape(m, D_MODEL)
    xc = c - jnp.mean(c, axis=-1, keepdims=True)
    cn = xc * lax.rsqrt(jnp.mean(xc * xc, axis=-1, keepdims=True) + EPS) * lng_ref[...] + lnb_ref[...]
    y = _dot(_silu(cn).astype(BF16), w2_ref[...]) + b2_ref[...] + x
    out_ref[...] = y.reshape(ls, tb, D_MODEL)


def _conformer_tm(x, hist, w, *, tb):
    ls, bsz, _ = x.shape
    seq_blk = lambda rows: pl.BlockSpec((rows, tb, D_MODEL), lambda i: (0, i, 0))
    return pl.pallas_call(
        functools.partial(_conformer_tm_kernel, ls=ls, tb=tb, cc=4 * LANES),
        grid=(bsz // tb,),
        in_specs=[
            seq_blk(ls),
            seq_blk(CONV_HIST),
            _const_spec((1, D_MODEL)),
            _const_spec((D_MODEL, 2 * D_MODEL)),
            _const_spec((1, 2 * D_MODEL)),
            _const_spec((CONV_WIDTH, SUBLANES, D_MODEL)),
            _const_spec((1, D_MODEL)),
            _const_spec((1, D_MODEL)),
            _const_spec((1, D_MODEL)),
            _const_spec((D_MODEL, D_MODEL)),
            _const_spec((1, D_MODEL)),
        ],
        out_specs=[seq_blk(ls), seq_blk(CONV_HIST)],
        out_shape=[
            jax.ShapeDtypeStruct((ls, bsz, D_MODEL), F32),
            jax.ShapeDtypeStruct((CONV_HIST, bsz, D_MODEL), F32),
        ],
        scratch_shapes=[
            pltpu.VMEM((CONV_HIST + ls, tb, D_MODEL), F32),
            pltpu.VMEM((ls, tb, D_MODEL), F32),
        ],
        compiler_params=_params(1),
        name="conformer_tm",
    )(x, hist, w["gm0"], w["w_pw1"], w["b_pw1"], w["w_dw"], w["b_dw"], w["ln_g"], w["ln_b"], w["w_pw2"], w["b_pw2"])


def _ffn_kernel(x_ref, g_ref, wg_ref, wu_ref, wd_ref, gf_ref, o_ref, *, final, cols):
    x = x_ref[...]
    hn = _rms(x, g_ref[...]).astype(BF16)
    y = x
    for lo in range(0, D_FF, cols):
        hi = min(lo + cols, D_FF)
        t = (_silu(_dot(hn, wg_ref[:, lo:hi])) * _dot(hn, wu_ref[:, lo:hi])).astype(BF16)
        y = y + _dot(t, wd_ref[lo:hi, :])
        if hi < D_FF:
            o_ref[...] = y
            y = o_ref[...]
    if final:
        y = _rms(y, gf_ref[...])
    o_ref[...] = y


def _ffn(x, g, wg, wu, wd, gf, *, tm, final):
    t = x.shape[0]
    cols = D_FF if tm <= 512 else FFN_COLS
    return pl.pallas_call(
        functools.partial(_ffn_kernel, final=final, cols=cols),
        grid=(t // tm,),
        in_specs=[
            pl.BlockSpec((tm, D_MODEL), lambda i: (i, 0)),
            _const_spec((1, D_MODEL)),
            _const_spec((D_MODEL, D_FF)),
            _const_spec((D_MODEL, D_FF)),
            _const_spec((D_FF, D_MODEL)),
            _const_spec((1, D_MODEL)),
        ],
        out_specs=pl.BlockSpec((tm, D_MODEL), lambda i: (i, 0)),
        out_shape=jax.ShapeDtypeStruct((t, D_MODEL), F32),
        compiler_params=_params(1),
        name="ffn_final" if final else "ffn",
    )(x, g, wg, wu, wd, gf)


def _layer0_kernel(h_ref, hist_ref, gm_ref, w1_ref, b1_ref, wdw_ref, bdw_ref, lng_ref, lnb_ref, w2_ref, b2_ref,
                   gf_ref, wg_ref, wu_ref, wd_ref, out_ref, nh_ref, gh_ref, c_ref, h1_ref, *, tl, rc, nl, nt):
    s = pl.program_id(0)
    slot = s % 2
    tile = jnp.minimum(s, nt - 1)

    @pl.when(s == 0)
    def _():
        h1_ref[1] = jnp.zeros((tl, D_MODEL), F32)

    @pl.when(tile % nl == 0)
    def _():
        gh_ref[:, 0:HIST_PAD - CONV_HIST, :] = jnp.zeros((1, HIST_PAD - CONV_HIST, D_MODEL), F32)
        gh_ref[:, HIST_PAD - CONV_HIST:HIST_PAD, :] = hist_ref[...]

    x1 = h1_ref[1 - slot]
    hn1 = _rms(x1, gf_ref[...]).astype(BF16)
    out_ref[0] = x1

    def ffn_block(lo, hi):
        t = (_silu(_dot(hn1, wg_ref[:, lo:hi])) * _dot(hn1, wu_ref[:, lo:hi])).astype(BF16)
        out_ref[0] = out_ref[0] + _dot(t, wd_ref[lo:hi, :])

    x = h_ref[0]
    hn = _rms(x, gm_ref[...]).astype(BF16)
    u = _dot(hn, w1_ref[...]) + b1_ref[...]
    gh_ref[0, HIST_PAD:HIST_PAD + tl, :] = u[:, :D_MODEL] * _sigmoid(u[:, D_MODEL:])
    pieces = _dwconv(gh_ref, wdw_ref, bdw_ref, c_ref, tb=1, tl=tl, rc=rc, cc=LANES, taps=CONV_WIDTH,
                     base=HIST_PAD - CONV_HIST, act=lambda v: v, static="deferred")
    blocks = [(lo, min(lo + FFN_COLS, D_FF)) for lo in range(0, D_FF, FFN_COLS)]
    per_block = -(-len(pieces) // len(blocks))
    @pl.when(s >= 0)
    def _():
        for j, (lo, hi) in enumerate(blocks):
            ffn_block(lo, hi)
            for p in pieces[j * per_block:(j + 1) * per_block]:
                p()
    nh_ref[...] = gh_ref[:, tl + HIST_PAD - CONV_HIST:tl + HIST_PAD, :]
    gh_ref[:, 0:HIST_PAD, :] = gh_ref[:, tl:tl + HIST_PAD, :]
    c = c_ref[0]
    xc = c - jnp.mean(c, axis=-1, keepdims=True)
    cn = xc * lax.rsqrt(jnp.mean(xc * xc, axis=-1, keepdims=True) + EPS) * lng_ref[...] + lnb_ref[...]
    h1_ref[slot] = _dot(_silu(cn).astype(BF16), w2_ref[...]) + b2_ref[...] + x


def _layer0(h, hist, w, *, tl):
    bsz, seq, _ = h.shape
    nl = seq // tl
    nt = bsz * nl
    cur = lambda s: jnp.minimum(s, nt - 1)
    prev = lambda s: jnp.maximum(s - 1, 0)
    return pl.pallas_call(
        functools.partial(_layer0_kernel, tl=tl, rc=CONV_PIECE_ROWS, nl=nl, nt=nt),
        grid=(nt + 1,),
        in_specs=[
            pl.BlockSpec((1, tl, D_MODEL), lambda s: (cur(s) // nl, cur(s) % nl, 0)),
            pl.BlockSpec((1, CONV_HIST, D_MODEL), lambda s: (0 if hist.shape[0] == 1 else cur(s) // nl, 0, 0)),
            _const_spec((1, D_MODEL)),
            _const_spec((D_MODEL, 2 * D_MODEL)),
            _const_spec((1, 2 * D_MODEL)),
            _const_spec((CONV_WIDTH, SUBLANES, D_MODEL)),
            _const_spec((1, D_MODEL)),
            _const_spec((1, D_MODEL)),
            _const_spec((1, D_MODEL)),
            _const_spec((D_MODEL, D_MODEL)),
            _const_spec((1, D_MODEL)),
            _const_spec((1, D_MODEL)),
            _const_spec((D_MODEL, D_FF)),
            _const_spec((D_MODEL, D_FF)),
            _const_spec((D_FF, D_MODEL)),
        ],
        out_specs=[
            pl.BlockSpec((1, tl, D_MODEL), lambda s: (prev(s) // nl, prev(s) % nl, 0)),
            pl.BlockSpec((1, CONV_HIST, D_MODEL), lambda s: (cur(s) // nl, 0, 0)),
        ],
        out_shape=[
            jax.ShapeDtypeStruct((bsz, seq, D_MODEL), F32),
            jax.ShapeDtypeStruct((bsz, CONV_HIST, D_MODEL), F32),
        ],
        scratch_shapes=[
            pltpu.VMEM((1, HIST_PAD + tl, D_MODEL), F32),
            pltpu.VMEM((1, tl, D_MODEL), F32),
            pltpu.VMEM((2, tl, D_MODEL), F32),
        ],
        compiler_params=_params(1),
        name="layer0",
    )(h, hist, w["gm0"], w["w_pw1"], w["b_pw1"], w["w_dw"], w["b_dw"], w["ln_g"], w["ln_b"], w["w_pw2"], w["b_pw2"],
      w["gf0"], w["wg0"], w["wu0"], w["wd0"])


def _softplus(x):
    return jnp.maximum(x, 0.0) + jnp.log(1.0 + jnp.exp(-jnp.abs(x)))


def _ssd_front_kernel(h_ref, ch_ref, gm_ref, wz_ref, wxa_ref, wxb_ref, wdt_ref, cw_ref, cb_ref, dtb_ref,
                      z_ref, xbc_ref, dt_ref, nc_ref, hist_ref, *, tb, tl):
    l = pl.program_id(1)
    m = tb * tl

    @pl.when(l == 0)
    def _():
        hist_ref[:, 0:SSD_HIST_PAD - SSD_CONV_HIST, :] = jnp.zeros((tb, SSD_HIST_PAD - SSD_CONV_HIST, CONV_DIM), F32)
        hist_ref[:, SSD_HIST_PAD - SSD_CONV_HIST:, :] = ch_ref[...]

    x = h_ref[...].reshape(m, D_MODEL)
    hn = _rms(x, gm_ref[...]).astype(BF16)
    dt_ref[...] = _softplus(_dot(hn, wdt_ref[...]) + dtb_ref[...]).reshape(tb, tl, LANES)
    row_in_group = lax.broadcasted_iota(jnp.int32, (tb, tl, FRONT_COLS), 1) % SUBLANES

    def conv_block(xn, lo):
        cs = slice(lo, lo + FRONT_COLS)
        xh = jnp.concatenate([hist_ref[:, :, cs], xn], axis=1)
        tap = lambda k: jnp.tile(cw_ref[k, :, cs], (tl // SUBLANES, 1))[None]
        acc = cb_ref[:, cs][None] + xn * tap(SSD_CONV_WIDTH - 1)
        for d in range(1, SSD_CONV_WIDTH):
            rot = _roll_groups_down(xh, d)
            acc = acc + jnp.where(row_in_group < d, rot[:, :tl], rot[:, SUBLANES:]) * tap(SSD_CONV_WIDTH - 1 - d)
        xbc_ref[:, :, cs] = _silu(acc).astype(xbc_ref.dtype)
        hist_ref[:, :, cs] = xn[:, tl - SSD_HIST_PAD:]

    def z_block(zn, lo):
        z_ref[:, :, lo:lo + FRONT_COLS] = zn.astype(z_ref.dtype)

    tasks = []
    z_cols = list(range(0, D_INNER, FRONT_COLS))
    for i, lo in enumerate(range(0, CONV_DIM, FRONT_COLS)):
        w_ref, off = (wxa_ref, lo) if lo < D_INNER else (wxb_ref, lo - D_INNER)
        tasks.append((w_ref, off, conv_block, lo))
        if i % 2 == 1:
            tasks.append((wz_ref, z_cols[i // 2], z_block, z_cols[i // 2]))
    pending = None
    for w_ref, off, finish, lo in tasks:
        res = _dot(hn, w_ref[:, off:off + FRONT_COLS]).reshape(tb, tl, FRONT_COLS)
        if pending is not None:
            pending[0](pending[1], pending[2])
        pending = (finish, res, lo)
    pending[0](pending[1], pending[2])

    @pl.when(l == pl.num_programs(1) - 1)
    def _():
        nc_ref[...] = hist_ref[:, SSD_HIST_PAD - SSD_CONV_HIST:, :]


def _ssd_front(h, ch, gm, w_in, wdt, cw, cb, dtb, *, tb, tl):
    bsz, seq, _ = h.shape
    act_dtype = BF16 if tl % (2 * SUBLANES) == 0 else F32
    w_blk = lambda j: pl.BlockSpec((D_MODEL, D_INNER), lambda *_: (0, j), pipeline_mode=pl.Buffered(1))
    return pl.pallas_call(
        functools.partial(_ssd_front_kernel, tb=tb, tl=tl),
        grid=(bsz // tb, seq // tl),
        in_specs=[
            pl.BlockSpec((tb, tl, D_MODEL), lambda b, l: (b, l, 0)),
            pl.BlockSpec((tb, SSD_CONV_HIST, CONV_DIM), lambda b, l: (0 if ch.shape[0] == 1 else b, 0, 0)),
            _const_spec((1, D_MODEL)),
            w_blk(0),
            w_blk(1),
            w_blk(2),
            _const_spec((D_MODEL, LANES)),
            _const_spec((SSD_CONV_WIDTH, SUBLANES, CONV_DIM)),
            _const_spec((1, CONV_DIM)),
            _const_spec((1, LANES)),
        ],
        out_specs=[
            pl.BlockSpec((tb, tl, D_INNER), lambda b, l: (b, l, 0)),
            pl.BlockSpec((tb, tl, CONV_DIM), lambda b, l: (b, l, 0)),
            pl.BlockSpec((tb, tl, LANES), lambda b, l: (b, l, 0)),
            pl.BlockSpec((tb, SSD_CONV_HIST, CONV_DIM), lambda b, l: (b, 0, 0)),
        ],
        out_shape=[
            jax.ShapeDtypeStruct((bsz, seq, D_INNER), act_dtype),
            jax.ShapeDtypeStruct((bsz, seq, CONV_DIM), act_dtype),
            jax.ShapeDtypeStruct((bsz, seq, LANES), F32),
            jax.ShapeDtypeStruct((bsz, SSD_CONV_HIST, CONV_DIM), F32),
        ],
        scratch_shapes=[pltpu.VMEM((tb, SSD_HIST_PAD, CONV_DIM), F32)],
        compiler_params=_params(2),
        name="ssd_front",
    )(h, ch, gm, w_in, w_in, w_in, wdt, cw, cb, dtb)


def _ssd_front_tm_kernel(x_ref, ch_ref, gm_ref, wz_ref, wxa_ref, wxb_ref, wdt_ref, cw_ref, cb_ref, dtb_ref,
                         z_ref, xbc_ref, dt_ref, nc_ref, *, ls, tb):
    m = ls * tb
    hn = _rms(x_ref[...].reshape(m, D_MODEL), gm_ref[...]).astype(BF16)
    dt_ref[...] = _softplus(_dot(hn, wdt_ref[...]) + dtb_ref[...]).reshape(ls, tb, LANES)
    z_ref[...] = _dot(hn, wz_ref[...]).reshape(ls, tb, D_INNER)
    for lo in range(0, CONV_DIM, FRONT_COLS):
        cs = slice(lo, lo + FRONT_COLS)
        w_ref, off = (wxa_ref, lo) if lo < D_INNER else (wxb_ref, lo - D_INNER)
        xn = _dot(hn, w_ref[:, off:off + FRONT_COLS]).reshape(ls, tb, FRONT_COLS)
        xh = jnp.concatenate([ch_ref[:, :, cs], xn], axis=0)
        acc = cb_ref[:, cs][None]
        for k in range(SSD_CONV_WIDTH):
            acc = acc + xh[k:k + ls] * jnp.tile(cw_ref[k, :, cs], (tb // SUBLANES, 1))[None]
        xbc_ref[:, :, cs] = _silu(acc)
        nc_ref[:, :, cs] = xh[ls:ls + SSD_CONV_HIST]


def _ssd_front_tm(x, ch, w, *, tb):
    ls, bsz, _ = x.shape
    w_in = w["w_in"]
    w_blk = lambda j: pl.BlockSpec((D_MODEL, D_INNER), lambda *_: (0, j), pipeline_mode=pl.Buffered(1))
    seq_blk = lambda rows, width: pl.BlockSpec((rows, tb, width), lambda i: (0, i, 0))
    return pl.pallas_call(
        functools.partial(_ssd_front_tm_kernel, ls=ls, tb=tb),
        grid=(bsz // tb,),
        in_specs=[
            seq_blk(ls, D_MODEL),
            seq_blk(SSD_CONV_HIST, CONV_DIM),
            _const_spec((1, D_MODEL)),
            w_blk(0),
            w_blk(1),
            w_blk(2),
            _const_spec((D_MODEL, LANES)),
            _const_spec((SSD_CONV_WIDTH, SUBLANES, CONV_DIM)),
            _const_spec((1, CONV_DIM)),
            _const_spec((1, LANES)),
        ],
        out_specs=[seq_blk(ls, D_INNER), seq_blk(ls, CONV_DIM), seq_blk(ls, LANES), seq_blk(SSD_CONV_HIST, CONV_DIM)],
        out_shape=[
            jax.ShapeDtypeStruct((ls, bsz, D_INNER), F32),
            jax.ShapeDtypeStruct((ls, bsz, CONV_DIM), F32),
            jax.ShapeDtypeStruct((ls, bsz, LANES), F32),
            jax.ShapeDtypeStruct((SSD_CONV_HIST, bsz, CONV_DIM), F32),
        ],
        compiler_params=_params(1),
        name="ssd_front_tm",
    )(x, ch, w["gm1"], w_in, w_in, w_in, w["w_dt"], w["conv_w"], w["conv_b"], w["dt_bias"])


def _cumsum_rows(a, tri):
    a_hi = a.astype(BF16)
    r1 = a - a_hi.astype(F32)
    a_mid = r1.astype(BF16)
    a_lo = (r1 - a_mid.astype(F32)).astype(BF16)
    return _dot(tri, a_hi) + _dot(tri, a_mid) + _dot(tri, a_lo)


def _head_rows(mat, g, width):
    return jnp.concatenate(
        [jnp.broadcast_to(mat[HEADS_PER_GROUP * g + r:HEADS_PER_GROUP * g + r + 1, :], (HEAD_DIM, width))
         for r in range(HEADS_PER_GROUP)], axis=0)


def _scan_fns(xbc_ref, z_ref, st_ref, vn_ref, alog_ref, dske_ref, ng_ref, lc):
    row = lax.broadcasted_iota(jnp.int32, (lc, lc), 0)
    col = lax.broadcasted_iota(jnp.int32, (lc, lc), 1)
    causal = row >= col
    tri = jnp.where(causal, 1.0, 0.0).astype(BF16)
    lane_head = lax.broadcasted_iota(jnp.int32, (lc, GROUP_DIM), 1) // HEAD_DIM
    lane_lo = lax.broadcasted_iota(jnp.int32, (lc, LANES), 1) < HEAD_DIM
    neg_a = -jnp.exp(alog_ref[...])

    def group_dots(i, r0, g):
        b_off = D_INNER + g * D_STATE
        c_off = D_INNER + N_GROUPS * D_STATE + g * D_STATE
        bg = xbc_ref[i, r0:r0 + lc, b_off:b_off + D_STATE].astype(BF16)
        cg = xbc_ref[i, r0:r0 + lc, c_off:c_off + D_STATE].astype(BF16)
        cb = lax.dot_general(cg, bg, _NT, preferred_element_type=F32)
        hg = st_ref[i, g]
        yoff = lax.dot_general(cg, hg.astype(BF16), _NT, preferred_element_type=F32)
        return bg, cb, hg, yoff

    def decays(dt):
        acs = _cumsum_rows(dt * neg_a, tri)
        acs_t = acs.T
        dt_t = dt.T
        last_t = jnp.broadcast_to(acs_t[:, lc - 1:lc], (LANES, lc))
        c_t = acs_t - jnp.log(dt_t)
        coef_t = jnp.exp(last_t - acs_t) * dt_t
        cd = jnp.broadcast_to(jnp.exp(acs_t[:, lc - 1:lc]), (LANES, D_STATE))
        return acs, c_t, coef_t, cd

    def group_tail(i, r0, g, dots, dec):
        bg, cb, hg, yoff = dots
        acs, c_t, coef_t, cd = dec
        lo, hi = g * GROUP_DIM, (g + 1) * GROUP_DIM
        xg = xbc_ref[i, r0:r0 + lc, lo:hi].astype(F32)
        ws, acs_b = [], []
        for r in range(HEADS_PER_GROUP):
            h = HEADS_PER_GROUP * g + r
            ab = jnp.broadcast_to(acs[:, h:h + 1], (lc, LANES))
            acs_b.append(ab)
            ws.append((cb * jnp.exp(jnp.where(causal, ab - c_t[h:h + 1, :], -jnp.inf))).astype(BF16))
        x_bd = jnp.concatenate([jnp.where(lane_head == r, xg, 0.0).astype(BF16) for r in range(HEADS_PER_GROUP)],
                               axis=0)
        yd = _dot(jnp.concatenate(ws, axis=1), x_bd)
        e_b = jnp.exp(jnp.concatenate([jnp.where(lane_lo, acs_b[0], acs_b[1]),
                                       jnp.where(lane_lo, acs_b[2], acs_b[3])], axis=1))
        y = yd + e_b * yoff + dske_ref[:, lo:hi] * xg

        xcoef_t = (xg.T * _head_rows(coef_t, g, lc)).astype(BF16)
        st_ref[i, g] = _head_rows(cd, g, D_STATE) * hg + _dot(xcoef_t, bg)

        v = y * _silu(z_ref[i, r0:r0 + lc, lo:hi].astype(F32))
        vn = v * lax.rsqrt(jnp.mean(v * v, axis=-1, keepdims=True) + EPS) * ng_ref[:, lo:hi]
        vn_ref[i, r0:r0 + lc, lo:hi] = vn.astype(BF16)

    return group_dots, decays, group_tail


def _ssd_scan_kernel(xbc_ref, dt_ref, dtn_ref, z_ref, h_ref, h0_ref, alog_ref, dske_ref, ng_ref, wo_ref, o_ref, st_ref,
                     vn_ref, dec_ref, *, lc, n_chunks):
    c = pl.program_id(1)
    first_slot = 0 if n_chunks % 2 == 0 else c % 2
    group_dots, decays, group_tail = _scan_fns(xbc_ref, z_ref, st_ref, vn_ref, alog_ref, dske_ref, ng_ref, lc)

    @pl.when(c == 0)
    def _():
        st_ref[...] = h0_ref[...]
        for k, v in enumerate(decays(dt_ref[0, 0:lc, :])):
            dec_ref[0, k] = v

    half = D_INNER // 2
    for ci in range(n_chunks):
        r0 = ci * lc
        slot = (first_slot + ci) % 2
        dots = group_dots(0, r0, 0)
        dec = tuple(dec_ref[slot, k] for k in range(4))
        for g in range(N_GROUPS):
            nxt = group_dots(0, r0, g + 1) if g + 1 < N_GROUPS else None
            group_tail(0, r0, g, dots, dec)
            dots = nxt
            if g == 1:
                next_dt = dt_ref[0, r0 + lc:r0 + 2 * lc, :] if ci + 1 < n_chunks else dtn_ref[0]
                for k, v in enumerate(decays(next_dt)):
                    dec_ref[1 - slot, k] = v
            if g == N_GROUPS // 2 - 1:
                o_ref[0, r0:r0 + lc, :] = h_ref[0, r0:r0 + lc, :] + _dot(vn_ref[0, r0:r0 + lc, :half], wo_ref[:half, :])
        o_ref[0, r0:r0 + lc, :] = o_ref[0, r0:r0 + lc, :] + _dot(vn_ref[0, r0:r0 + lc, half:], wo_ref[half:, :])


def _ssd_scan(xbc, dt, z, h, h0, alog, dske, ng, wo, *, lc):
    bsz, seq, _ = xbc.shape
    assert lc == LANES
    nb = 1
    n_chunks = SCAN_CHUNKS if seq % (SCAN_CHUNKS * lc) == 0 else 1
    rows = n_chunks * lc
    last = seq // lc - 1
    shared_h0 = h0.shape[0] == 1
    return pl.pallas_call(
        functools.partial(_ssd_scan_kernel, lc=lc, n_chunks=n_chunks),
        grid=(bsz // nb, seq // rows),
        in_specs=[
            pl.BlockSpec((nb, rows, CONV_DIM), lambda b, c: (b, c, 0)),
            pl.BlockSpec((nb, rows, LANES), lambda b, c: (b, c, 0)),
            pl.BlockSpec((nb, lc, LANES), lambda b, c: (b, jnp.minimum((c + 1) * n_chunks, last), 0)),
            pl.BlockSpec((nb, rows, D_INNER), lambda b, c: (b, c, 0)),
            pl.BlockSpec((nb, rows, D_MODEL), lambda b, c: (b, c, 0)),
            pl.BlockSpec((nb, N_GROUPS, GROUP_DIM, D_STATE), lambda b, c: (0 if shared_h0 else b, 0, 0, 0)),
            _const_spec((1, LANES)),
            _const_spec((1, D_INNER)),
            _const_spec((1, D_INNER)),
            _const_spec((D_INNER, D_MODEL)),
        ],
        out_specs=[
            pl.BlockSpec((nb, rows, D_MODEL), lambda b, c: (b, c, 0)),
            pl.BlockSpec((nb, N_GROUPS, GROUP_DIM, D_STATE), lambda b, c: (b, 0, 0, 0)),
        ],
        out_shape=[
            jax.ShapeDtypeStruct((bsz, seq, D_MODEL), F32),
            jax.ShapeDtypeStruct((bsz, N_GROUPS, GROUP_DIM, D_STATE), F32),
        ],
        scratch_shapes=[pltpu.VMEM((nb, rows, D_INNER), BF16), pltpu.VMEM((2, 4, LANES, LANES), F32)],
        compiler_params=_params(2),
        name="ssd_scan",
    )(xbc, dt, dt, z, h, h0, alog, dske, ng, wo)


def _ssd_step_kernel(x_ref, b_ref, c_ref, dt_ref, st_ref, alog_ref, dsk_ref, y_ref, so_ref, *, nb, ls):
    g = pl.program_id(1)
    rows = nb * ls
    shift = (LANES - HEADS_PER_GROUP * g) % LANES
    dt = pltpu.roll(dt_ref[...].reshape(rows, LANES), shift, axis=1)
    alog = pltpu.roll(alog_ref[...], shift, axis=1)
    dsk = pltpu.roll(dsk_ref[...], shift, axis=1)
    a = dt * (-jnp.exp(alog))

    row = lax.broadcasted_iota(jnp.int32, (rows, rows), 0)
    col = lax.broadcasted_iota(jnp.int32, (rows, rows), 1)
    same = (row % nb) == (col % nb)
    causal = jnp.logical_and(same, row >= col)
    hi = lax.Precision.HIGHEST
    acs = jnp.dot(jnp.where(causal, 1.0, 0.0), a, preferred_element_type=F32, precision=hi)
    tot = jnp.dot(jnp.where(same, 1.0, 0.0), a, preferred_element_type=F32, precision=hi)
    acs_t = acs.T
    coef = jnp.exp(tot - acs) * dt
    eacs = jnp.exp(acs)
    etot = jnp.exp(tot)

    bg = b_ref[...].reshape(rows, D_STATE).astype(BF16)
    cg = c_ref[...].reshape(rows, D_STATE).astype(BF16)
    xg = x_ref[...].reshape(rows, GROUP_DIM)
    cb = lax.dot_general(cg, bg, _NT, preferred_element_type=F32)

    seq_of_col = lax.broadcasted_iota(jnp.int32, (GROUP_DIM, rows), 1) % nb
    yoff_t = jnp.zeros((GROUP_DIM, rows), F32)
    for b in range(nb):
        yb = lax.dot_general(st_ref[b, 0].astype(BF16), cg, _NT, preferred_element_type=F32)
        yoff_t = jnp.where(seq_of_col == b, yb, yoff_t)
    yoff = yoff_t.T

    ys, xcs = [], []
    for r in range(HEADS_PER_GROUP):
        decay = jnp.exp(jnp.where(causal, acs[:, r:r + 1] - acs_t[r:r + 1, :], -jnp.inf))
        w = (cb * decay).astype(BF16)
        xr = xg[:, r * HEAD_DIM:(r + 1) * HEAD_DIM]
        yd = _dot(w, (xr * dt[:, r:r + 1]).astype(BF16))
        ys.append(yd + eacs[:, r:r + 1] * yoff[:, r * HEAD_DIM:(r + 1) * HEAD_DIM] + dsk[:, r:r + 1] * xr)
        xcs.append(xr * coef[:, r:r + 1])
    y_ref[...] = jnp.concatenate(ys, axis=1).reshape(ls, nb, GROUP_DIM)
    xcoef_t = jnp.concatenate(xcs, axis=1).T

    for b in range(nb):
        upd = _dot(jnp.where(seq_of_col == b, xcoef_t, 0.0).astype(BF16), bg)
        scale = jnp.concatenate(
            [jnp.broadcast_to(etot[b:b + 1, r:r + 1], (HEAD_DIM, D_STATE)) for r in range(HEADS_PER_GROUP)],
            axis=0)
        so_ref[b, 0] = scale * st_ref[b, 0] + upd


def _ssd_step(xbc, dt, st, alog, dsk, *, nb, ls):
    n_seq = st.shape[0]
    b_blk0 = D_INNER // D_STATE
    c_blk0 = b_blk0 + N_GROUPS
    return pl.pallas_call(
        functools.partial(_ssd_step_kernel, nb=nb, ls=ls),
        grid=(n_seq // nb, N_GROUPS),
        in_specs=[
            pl.BlockSpec((ls, nb, GROUP_DIM), lambda i, g: (0, i, g)),
            pl.BlockSpec((ls, nb, D_STATE), lambda i, g: (0, i, b_blk0 + g)),
            pl.BlockSpec((ls, nb, D_STATE), lambda i, g: (0, i, c_blk0 + g)),
            pl.BlockSpec((ls, nb, LANES), lambda i, g: (0, i, 0)),
            pl.BlockSpec((nb, 1, GROUP_DIM, D_STATE), lambda i, g: (i, g, 0, 0)),
            _const_spec((1, LANES)),
            _const_spec((1, LANES)),
        ],
        out_specs=[
            pl.BlockSpec((ls, nb, GROUP_DIM), lambda i, g: (0, i, g)),
            pl.BlockSpec((nb, 1, GROUP_DIM, D_STATE), lambda i, g: (i, g, 0, 0)),
        ],
        out_shape=[
            jax.ShapeDtypeStruct((ls, n_seq, D_INNER), F32),
            jax.ShapeDtypeStruct((n_seq, N_GROUPS, GROUP_DIM, D_STATE), F32),
        ],
        compiler_params=_params(2),
        name="ssd_step",
    )(xbc, xbc, xbc, dt, st, alog, dsk)


def _ssd_post_kernel(y_ref, z_ref, h_ref, ng_ref, wo_ref, o_ref):
    acc = h_ref[...]
    for g in range(N_GROUPS):
        lo, hi = g * GROUP_DIM, (g + 1) * GROUP_DIM
        v = y_ref[:, lo:hi] * _silu(z_ref[:, lo:hi])
        vn = v * lax.rsqrt(jnp.mean(v * v, axis=-1, keepdims=True) + EPS) * ng_ref[:, lo:hi]
        acc = acc + _dot(vn.astype(BF16), wo_ref[lo:hi, :])
    o_ref[...] = acc


def _ssd_post(y, z, h, ng, wo, *, tm):
    t = y.shape[0]
    return pl.pallas_call(
        _ssd_post_kernel,
        grid=(t // tm,),
        in_specs=[
            pl.BlockSpec((tm, D_INNER), lambda i: (i, 0)),
            pl.BlockSpec((tm, D_INNER), lambda i: (i, 0)),
            pl.BlockSpec((tm, D_MODEL), lambda i: (i, 0)),
            _const_spec((1, D_INNER)),
            _const_spec((D_INNER, D_MODEL)),
        ],
        out_specs=pl.BlockSpec((tm, D_MODEL), lambda i: (i, 0)),
        out_shape=jax.ShapeDtypeStruct((t, D_MODEL), F32),
        compiler_params=_params(1),
        name="ssd_post",
    )(y, z, h, ng, wo)


def _trunk(h, cc_hist, sc_hist, ssm, w, *, tb, tl, tm, tl_front=None):
    bsz, seq, _ = h.shape
    t = bsz * seq
    if tb == 1 and tl % 64 == 0:
        h2, new_cc = _layer0(h, cc_hist, w, tl=tl)
        h2 = h2.reshape(t, D_MODEL)
    else:
        h1, new_cc = _conformer(h, cc_hist, w["gm0"], w["w_pw1"], w["b_pw1"], w["w_dw"], w["b_dw"], w["ln_g"],
                                w["ln_b"], w["w_pw2"], w["b_pw2"], tb=tb, tl=tl)
        h2 = _ffn(h1.reshape(t, D_MODEL), w["gf0"], w["wg0"], w["wu0"], w["wd0"], w["g_final"], tm=tm, final=False)
    z, xbc, dt, new_sc = _ssd_front(h2.reshape(bsz, seq, D_MODEL), sc_hist, w["gm1"], w["w_in"], w["w_dt"],
                                    w["conv_w"], w["conv_b"], w["dt_bias"], tb=tb, tl=tl_front or tl)
    pad = (-seq) % LANES
    rows = lambda v: jnp.pad(v, ((0, 0), (0, pad), (0, 0))) if pad else v
    h3, new_ss = _ssd_scan(rows(xbc), rows(dt), rows(z), rows(h2.reshape(bsz, seq, D_MODEL)), ssm,
                           w["a_log"], w["d_skip_lanes"], w["norm_g"], w["w_out"], lc=LANES)
    h3 = h3[:, :seq].reshape(t, D_MODEL)
    out = _ffn(h3, w["gf1"], w["wg1"], w["wu1"], w["wd1"], w["g_final"], tm=tm, final=True)
    return out.reshape(bsz, seq, D_MODEL), new_cc, new_sc, new_ss


def _trunk_short(x, cc_hist, sc_hist, ssm, w, *, tb, tm):
    ls, bsz, _ = x.shape
    t = ls * bsz
    h1, new_cc = _conformer_tm(x, cc_hist, w, tb=tb)
    h2 = _ffn(h1.reshape(t, D_MODEL), w["gf0"], w["wg0"], w["wu0"], w["wd0"], w["g_final"], tm=tm, final=False)
    z, xbc, dt, new_sc = _ssd_front_tm(h2.reshape(ls, bsz, D_MODEL), sc_hist, w, tb=tb)
    y, new_ss = _ssd_step(xbc, dt, ssm, w["a_log"], w["d_skip"], nb=LANES // ls, ls=ls)
    h3 = _ssd_post(y.reshape(t, D_INNER), z.reshape(t, D_INNER), h2, w["norm_g"], w["w_out"], tm=tm)
    out = _ffn(h3, w["gf1"], w["wg1"], w["wu1"], w["wd1"], w["g_final"], tm=tm, final=True)
    return out.reshape(ls, bsz, D_MODEL), new_cc, new_sc, new_ss


def _pad_lanes(v):
    return jnp.pad(v.astype(F32), (0, LANES - v.shape[0])).reshape(1, LANES)


def kernel(x_prompt, x_sample, cache_conv, state_ssd_conv, state_ssm, meta_tokens, norm_mix, norm_ffn, norm_final, cf_w_pw1, cf_b_pw1, cf_w_dw, cf_b_dw, cf_ln_g, cf_ln_b, cf_w_pw2, cf_b_pw2, ssd_w_in, ssd_conv_w, ssd_conv_b, ssd_dt_bias, ssd_a_log, ssd_d, ssd_norm_g, ssd_w_out, ffn_w_gate, ffn_w_up, ffn_w_down):
    row = lambda v: v.astype(F32).reshape(1, -1)
    w_in = ssd_w_in[0]
    w = {
        "gm0": row(norm_mix[0]), "gm1": row(norm_mix[1]),
        "gf0": row(norm_ffn[0]), "gf1": row(norm_ffn[1]), "g_final": row(norm_final),
        "w_pw1": cf_w_pw1[0].astype(BF16), "b_pw1": row(cf_b_pw1[0]),
        "w_dw": jnp.broadcast_to(cf_w_dw[0][:, None, :], (CONV_WIDTH, SUBLANES, D_MODEL)),
        "b_dw": row(cf_b_dw[0]), "ln_g": row(cf_ln_g[0]), "ln_b": row(cf_ln_b[0]),
        "w_pw2": cf_w_pw2[0].astype(BF16), "b_pw2": row(cf_b_pw2[0]),
        "w_in": w_in.astype(BF16),
        "w_dt": jnp.pad(w_in[:, D_INNER + CONV_DIM:], ((0, 0), (0, LANES - N_HEADS))).astype(BF16),
        "conv_w": jnp.broadcast_to(ssd_conv_w[0][:, None, :], (SSD_CONV_WIDTH, SUBLANES, CONV_DIM)),
        "conv_b": row(ssd_conv_b[0]),
        "dt_bias": _pad_lanes(ssd_dt_bias[0]), "a_log": _pad_lanes(ssd_a_log[0]), "d_skip": _pad_lanes(ssd_d[0]),
        "d_skip_lanes": jnp.repeat(ssd_d[0].astype(F32), HEAD_DIM).reshape(1, D_INNER),
        "norm_g": row(ssd_norm_g[0]), "w_out": ssd_w_out[0].astype(BF16),
        "wg0": ffn_w_gate[0].astype(BF16), "wu0": ffn_w_up[0].astype(BF16), "wd0": ffn_w_down[0].astype(BF16),
        "wg1": ffn_w_gate[1].astype(BF16), "wu1": ffn_w_up[1].astype(BF16), "wd1": ffn_w_down[1].astype(BF16),
    }
    bp, seq, _ = x_prompt.shape
    bs = x_sample.shape[0]

    _, cc_m, sc_m, ss_m = _trunk(
        meta_tokens.astype(F32)[None], jnp.zeros((1, CONV_HIST, D_MODEL), F32),
        jnp.zeros((1, SSD_CONV_HIST, CONV_DIM), F32), jnp.zeros((1, N_GROUPS, GROUP_DIM, D_STATE), F32), w,
        tb=1, tl=N_META, tm=N_META)

    y_prompt, cc_p, sc_p, ss_p = _trunk(
        x_prompt, cc_m, sc_m,
        ss_m, w, tb=1, tl=LAYER0_ROWS, tl_front=FRONT_ROWS, tm=FFN_ROWS)

    swap = lambda v: jnp.transpose(v, (1, 0, 2))
    y_s, cc_s, sc_s, ss_s = _trunk_short(
        swap(x_sample), swap(cache_conv[0]), swap(state_ssd_conv[0]),
        state_ssm[0].reshape(bs, N_GROUPS, GROUP_DIM, D_STATE), w, tb=SHORT_SEQS, tm=FFN_ROWS)

    unpack_ss = lambda v: v.reshape(1, v.shape[0], N_HEADS, HEAD_DIM, D_STATE)
    return (y_prompt, swap(y_s), cc_p[None], swap(cc_s)[None], sc_p[None], swap(sc_s)[None],
            unpack_ss(ss_p), unpack_ss(ss_s))
```

```python
import functools

import jax
import jax.numpy as jnp
from jax import lax
from jax.experimental import pallas as pl
from jax.experimental.pallas import tpu as pltpu

D_MODEL = 1024
N_META = 16
CONV_WIDTH = 31
CONV_HIST = CONV_WIDTH - 1
D_INNER = 2048
HEAD_DIM = 64
N_HEADS = 32
N_GROUPS = 8
HEADS_PER_GROUP = 4
GROUP_DIM = HEADS_PER_GROUP * HEAD_DIM
D_STATE = 128
SSD_CONV_WIDTH = 4
SSD_CONV_HIST = SSD_CONV_WIDTH - 1
CONV_DIM = D_INNER + 2 * N_GROUPS * D_STATE
D_FF = 2816
EPS = 1e-6

LANES = 128
SUBLANES = 8
HIST_PAD = 32
SSD_HIST_PAD = 8
FRONT_COLS = 256
SCAN_CHUNKS = 4
FFN_COLS = 512
CONV_PIECE_ROWS = 64
LAYER0_ROWS = 256
FRONT_ROWS = 1024
FFN_ROWS = 512
SHORT_SEQS = 32
STEP_GROUPS = 4
V7X_VMEM_BYTES = 64 * 1024 * 1024
VMEM_LIMIT = V7X_VMEM_BYTES - 8 * 1024 * 1024

F32 = jnp.float32
BF16 = jnp.bfloat16
_NT = (((1,), (1,)), ((), ()))


def _const_spec(shape):
    return pl.BlockSpec(shape, lambda *_: (0,) * len(shape), pipeline_mode=pl.Buffered(1))


def _params(n_axes):
    return pltpu.CompilerParams(dimension_semantics=("arbitrary",) * n_axes, vmem_limit_bytes=VMEM_LIMIT)


def _rms(x, g):
    return x * lax.rsqrt(jnp.mean(x * x, axis=-1, keepdims=True) + EPS) * g


def _sigmoid(x):
    return 1.0 / (1.0 + jnp.exp(-x))


def _silu(x):
    return x * _sigmoid(x)


def _dot(a, b):
    return jnp.dot(a, b, preferred_element_type=F32)


def _roll_groups_down(x, d):
    b, n, c = x.shape
    return pltpu.roll(x.reshape(b * n // SUBLANES, SUBLANES, c), d, axis=1).reshape(b, n, c)


def _shift_rows_up(x, s):
    n = x.shape[0] - SUBLANES
    rot = [pltpu.roll(x[j:j + SUBLANES], SUBLANES - s, axis=0) for j in range(0, n + SUBLANES, SUBLANES)]
    keep = lax.broadcasted_iota(jnp.int32, (SUBLANES, x.shape[1]), 0) < SUBLANES - s
    return jnp.concatenate([jnp.where(keep, rot[j], rot[j + 1]) for j in range(n // SUBLANES)], axis=0)


def _dwconv(src_ref, w_ref, bias_ref, dst_ref, *, tb, tl, rc, cc, taps, base, act, cols=None, static=False):
    n_rc = tl // rc
    c_lo, c_hi = cols if cols is not None else (0, src_ref.shape[-1])
    by_shift = {}
    for k in range(taps):
        by_shift.setdefault((base + k) % SUBLANES, []).append(k)

    def piece(b, r0, c0):
        acc = jnp.broadcast_to(bias_ref[:, c0:c0 + cc], (rc, cc))
        for s, ks in sorted(by_shift.items()):
            n_rows = rc + SUBLANES if s else rc
            part = None
            for k in ks:
                q = (base + k) // SUBLANES
                blk = src_ref[b, pl.ds(r0 + SUBLANES * q, n_rows), c0:c0 + cc]
                wk = jnp.tile(w_ref[k, :, c0:c0 + cc], (n_rows // SUBLANES, 1))
                part = blk * wk if part is None else part + blk * wk
            acc = acc + (_shift_rows_up(part, s) if s else part)
        dst_ref[b, pl.ds(r0, rc), c0:c0 + cc] = act(acc)

    def chunk(b, r0):
        for c0 in range(c_lo, c_hi, cc):
            piece(b, r0, c0)

    if static == "deferred":
        return [functools.partial(piece, b, j * rc, c0)
                for b in range(tb) for j in range(n_rc) for c0 in range(c_lo, c_hi, cc)]
    if static:
        for b in range(tb):
            for j in range(n_rc):
                chunk(b, j * rc)
    else:
        def body(i, carry):
            chunk(i // n_rc, pl.multiple_of((i % n_rc) * rc, SUBLANES))
            return carry

        lax.fori_loop(0, tb * n_rc, body, 0)


def _conformer_kernel(h_ref, hist_ref, gm_ref, w1_ref, b1_ref, wdw_ref, bdw_ref, lng_ref, lnb_ref, w2_ref, b2_ref,
                      out_ref, nh_ref, gh_ref, c_ref, *, tb, tl, rc):
    l = pl.program_id(1)
    m = tb * tl

    @pl.when(l == 0)
    def _():
        gh_ref[:, 0:HIST_PAD - CONV_HIST, :] = jnp.zeros((tb, HIST_PAD - CONV_HIST, D_MODEL), F32)
        gh_ref[:, HIST_PAD - CONV_HIST:HIST_PAD, :] = hist_ref[...]

    x = h_ref[...].reshape(m, D_MODEL)
    hn = _rms(x, gm_ref[...]).astype(BF16)
    u = _dot(hn, w1_ref[...]) + b1_ref[...]
    g = u[:, :D_MODEL] * _sigmoid(u[:, D_MODEL:])
    gh_ref[:, HIST_PAD:HIST_PAD + tl, :] = g.reshape(tb, tl, D_MODEL)

    _dwconv(gh_ref, wdw_ref, bdw_ref, c_ref, tb=tb, tl=tl, rc=rc, cc=LANES if rc >= 64 else 4 * LANES,
            taps=CONV_WIDTH, base=HIST_PAD - CONV_HIST, act=lambda v: v)

    @pl.when(l == pl.num_programs(1) - 1)
    def _():
        nh_ref[...] = gh_ref[:, tl + HIST_PAD - CONV_HIST:tl + HIST_PAD, :]

    gh_ref[:, 0:HIST_PAD, :] = gh_ref[:, tl:tl + HIST_PAD, :]

    c = c_ref[...].reshape(m, D_MODEL)
    mu = jnp.mean(c, axis=-1, keepdims=True)
    xc = c - mu
    cn = xc * lax.rsqrt(jnp.mean(xc * xc, axis=-1, keepdims=True) + EPS) * lng_ref[...] + lnb_ref[...]
    cn = _silu(cn).astype(BF16)
    y = _dot(cn, w2_ref[...]) + b2_ref[...] + x
    out_ref[...] = y.reshape(tb, tl, D_MODEL)


def _conformer(h, hist, gm, w1, b1, wdw, bdw, lng, lnb, w2, b2, *, tb, tl):
    bsz, seq, _ = h.shape
    rc = 64 if tl % 64 == 0 else (16 if tl % 16 == 0 else SUBLANES)
    kern = functools.partial(_conformer_kernel, tb=tb, tl=tl, rc=rc)
    return pl.pallas_call(
        kern,
        grid=(bsz // tb, seq // tl),
        in_specs=[
            pl.BlockSpec((tb, tl, D_MODEL), lambda b, l: (b, l, 0)),
            pl.BlockSpec((tb, CONV_HIST, D_MODEL), lambda b, l: (b, 0, 0)),
            _const_spec((1, D_MODEL)),
            _const_spec((D_MODEL, 2 * D_MODEL)),
            _const_spec((1, 2 * D_MODEL)),
            _const_spec((CONV_WIDTH, SUBLANES, D_MODEL)),
            _const_spec((1, D_MODEL)),
            _const_spec((1, D_MODEL)),
            _const_spec((1, D_MODEL)),
            _const_spec((D_MODEL, D_MODEL)),
            _const_spec((1, D_MODEL)),
        ],
        out_specs=[
            pl.BlockSpec((tb, tl, D_MODEL), lambda b, l: (b, l, 0)),
            pl.BlockSpec((tb, CONV_HIST, D_MODEL), lambda b, l: (b, 0, 0)),
        ],
        out_shape=[
            jax.ShapeDtypeStruct((bsz, seq, D_MODEL), F32),
            jax.ShapeDtypeStruct((bsz, CONV_HIST, D_MODEL), F32),
        ],
        scratch_shapes=[
            pltpu.VMEM((tb, HIST_PAD + tl, D_MODEL), F32),
            pltpu.VMEM((tb, tl, D_MODEL), F32),
        ],
        compiler_params=_params(2),
        name="conformer",
    )(h, hist, gm, w1, b1, wdw, bdw, lng, lnb, w2, b2)


def _conformer_tm_kernel(x_ref, hist_ref, gm_ref, w1_ref, b1_ref, wdw_ref, bdw_ref, lng_ref, lnb_ref, w2_ref, b2_ref,
                         out_ref, nh_ref, gh_ref, c_ref, *, ls, tb, cc):
    m = ls * tb
    x = x_ref[...].reshape(m, D_MODEL)
    hn = _rms(x, gm_ref[...]).astype(BF16)
    u = _dot(hn, w1_ref[...]) + b1_ref[...]
    gh_ref[0:CONV_HIST] = hist_ref[...]
    gh_ref[CONV_HIST:CONV_HIST + ls] = (u[:, :D_MODEL] * _sigmoid(u[:, D_MODEL:])).reshape(ls, tb, D_MODEL)

    def conv_token(t, carry):
        for c0 in range(0, D_MODEL, cc):
            acc = jnp.broadcast_to(bdw_ref[:, c0:c0 + cc], (tb, cc))
            for k in range(CONV_WIDTH):
                acc = acc + gh_ref[t + k, :, c0:c0 + cc] * jnp.tile(wdw_ref[k, :, c0:c0 + cc], (tb // SUBLANES, 1))
            c_ref[t, :, c0:c0 + cc] = acc
        return carry

    lax.fori_loop(0, ls, conv_token, 0)
    nh_ref[...] = gh_ref[ls:ls + CONV_HIST]

    c = c_ref[...].reshape(m, D_MODEL)
    xc = c - jnp.mean(c, axis=-1, keepdims=True)
    cn = xc * lax.rsqrt(jnp.mean(xc * xc, axis=-1, keepdims=True) + EPS) * lng_ref[...] + lnb_ref[...]
    y = _dot(_silu(cn).astype(BF16), w2_ref[...]) + b2_ref[...] + x
    out_ref[...] = y.reshape(ls, tb, D_MODEL)


def _conformer_tm(x, hist, w, *, tb):
    ls, bsz, _ = x.shape
    seq_blk = lambda rows: pl.BlockSpec((rows, tb, D_MODEL), lambda i: (0, i, 0))
    return pl.pallas_call(
        functools.partial(_conformer_tm_kernel, ls=ls, tb=tb, cc=4 * LANES),
        grid=(bsz // tb,),
        in_specs=[
            seq_blk(ls),
            seq_blk(CONV_HIST),
            _const_spec((1, D_MODEL)),
            _const_spec((D_MODEL, 2 * D_MODEL)),
            _const_spec((1, 2 * D_MODEL)),
            _const_spec((CONV_WIDTH, SUBLANES, D_MODEL)),
            _const_spec((1, D_MODEL)),
            _const_spec((1, D_MODEL)),
            _const_spec((1, D_MODEL)),
            _const_spec((D_MODEL, D_MODEL)),
            _const_spec((1, D_MODEL)),
        ],
        out_specs=[seq_blk(ls), seq_blk(CONV_HIST)],
        out_shape=[
            jax.ShapeDtypeStruct((ls, bsz, D_MODEL), F32),
            jax.ShapeDtypeStruct((CONV_HIST, bsz, D_MODEL), F32),
        ],
        scratch_shapes=[
            pltpu.VMEM((CONV_HIST + ls, tb, D_MODEL), F32),
            pltpu.VMEM((ls, tb, D_MODEL), F32),
        ],
        compiler_params=_params(1),
        name="conformer_tm",
    )(x, hist, w["gm0"], w["w_pw1"], w["b_pw1"], w["w_dw"], w["b_dw"], w["ln_g"], w["ln_b"], w["w_pw2"], w["b_pw2"])


def _ffn_kernel(x_ref, g_ref, wg_ref, wu_ref, wd_ref, gf_ref, o_ref, *, final, cols):
    x = x_ref[...]
    hn = _rms(x, g_ref[...]).astype(BF16)
    y = x
    for lo in range(0, D_FF, cols):
        hi = min(lo + cols, D_FF)
        t = (_silu(_dot(hn, wg_ref[:, lo:hi])) * _dot(hn, wu_ref[:, lo:hi])).astype(BF16)
        y = y + _dot(t, wd_ref[lo:hi, :])
        if hi < D_FF:
            o_ref[...] = y
            y = o_ref[...]
    if final:
        y = _rms(y, gf_ref[...])
    o_ref[...] = y


def _ffn(x, g, wg, wu, wd, gf, *, tm, final):
    t = x.shape[0]
    cols = D_FF if tm <= 512 else FFN_COLS
    return pl.pallas_call(
        functools.partial(_ffn_kernel, final=final, cols=cols),
        grid=(t // tm,),
        in_specs=[
            pl.BlockSpec((tm, D_MODEL), lambda i: (i, 0)),
            _const_spec((1, D_MODEL)),
            _const_spec((D_MODEL, D_FF)),
            _const_spec((D_MODEL, D_FF)),
            _const_spec((D_FF, D_MODEL)),
            _const_spec((1, D_MODEL)),
        ],
        out_specs=pl.BlockSpec((tm, D_MODEL), lambda i: (i, 0)),
        out_shape=jax.ShapeDtypeStruct((t, D_MODEL), F32),
        compiler_params=_params(1),
        name="ffn_final" if final else "ffn",
    )(x, g, wg, wu, wd, gf)


def _layer0_kernel(h_ref, hist_ref, gm_ref, w1_ref, b1_ref, wdw_ref, bdw_ref, lng_ref, lnb_ref, w2_ref, b2_ref,
                   gf_ref, wg_ref, wu_ref, wd_ref, out_ref, nh_ref, gh_ref, c_ref, h1_ref, *, tl, rc, nl, nt):
    s = pl.program_id(0)
    slot = s % 2
    tile = jnp.minimum(s, nt - 1)

    @pl.when(s == 0)
    def _():
        h1_ref[1] = jnp.zeros((tl, D_MODEL), F32)

    @pl.when(tile % nl == 0)
    def _():
        gh_ref[:, 0:HIST_PAD - CONV_HIST, :] = jnp.zeros((1, HIST_PAD - CONV_HIST, D_MODEL), F32)
        gh_ref[:, HIST_PAD - CONV_HIST:HIST_PAD, :] = hist_ref[...]

    x1 = h1_ref[1 - slot]
    hn1 = _rms(x1, gf_ref[...]).astype(BF16)
    out_ref[0] = x1

    def ffn_block(lo, hi):
        t = (_silu(_dot(hn1, wg_ref[:, lo:hi])) * _dot(hn1, wu_ref[:, lo:hi])).astype(BF16)
        out_ref[0] = out_ref[0] + _dot(t, wd_ref[lo:hi, :])

    x = h_ref[0]
    hn = _rms(x, gm_ref[...]).astype(BF16)
    u = _dot(hn, w1_ref[...]) + b1_ref[...]
    gh_ref[0, HIST_PAD:HIST_PAD + tl, :] = u[:, :D_MODEL] * _sigmoid(u[:, D_MODEL:])
    pieces = _dwconv(gh_ref, wdw_ref, bdw_ref, c_ref, tb=1, tl=tl, rc=rc, cc=LANES, taps=CONV_WIDTH,
                     base=HIST_PAD - CONV_HIST, act=lambda v: v, static="deferred")
    blocks = [(lo, min(lo + FFN_COLS, D_FF)) for lo in range(0, D_FF, FFN_COLS)]
    per_block = -(-len(pieces) // len(blocks))
    @pl.when(s >= 0)
    def _():
        for j, (lo, hi) in enumerate(blocks):
            ffn_block(lo, hi)
            for p in pieces[j * per_block:(j + 1) * per_block]:
                p()
    nh_ref[...] = gh_ref[:, tl + HIST_PAD - CONV_HIST:tl + HIST_PAD, :]
    gh_ref[:, 0:HIST_PAD, :] = gh_ref[:, tl:tl + HIST_PAD, :]
    c = c_ref[0]
    xc = c - jnp.mean(c, axis=-1, keepdims=True)
    cn = xc * lax.rsqrt(jnp.mean(xc * xc, axis=-1, keepdims=True) + EPS) * lng_ref[...] + lnb_ref[...]
    h1_ref[slot] = _dot(_silu(cn).astype(BF16), w2_ref[...]) + b2_ref[...] + x


def _layer0(h, hist, w, *, tl):
    bsz, seq, _ = h.shape
    nl = seq // tl
    nt = bsz * nl
    cur = lambda s: jnp.minimum(s, nt - 1)
    prev = lambda s: jnp.maximum(s - 1, 0)
    return pl.pallas_call(
        functools.partial(_layer0_kernel, tl=tl, rc=CONV_PIECE_ROWS, nl=nl, nt=nt),
        grid=(nt + 1,),
        in_specs=[
            pl.BlockSpec((1, tl, D_MODEL), lambda s: (cur(s) // nl, cur(s) % nl, 0)),
            pl.BlockSpec((1, CONV_HIST, D_MODEL), lambda s: (0 if hist.shape[0] == 1 else cur(s) // nl, 0, 0)),
            _const_spec((1, D_MODEL)),
            _const_spec((D_MODEL, 2 * D_MODEL)),
            _const_spec((1, 2 * D_MODEL)),
            _const_spec((CONV_WIDTH, SUBLANES, D_MODEL)),
            _const_spec((1, D_MODEL)),
            _const_spec((1, D_MODEL)),
            _const_spec((1, D_MODEL)),
            _const_spec((D_MODEL, D_MODEL)),
            _const_spec((1, D_MODEL)),
            _const_spec((1, D_MODEL)),
            _const_spec((D_MODEL, D_FF)),
            _const_spec((D_MODEL, D_FF)),
            _const_spec((D_FF, D_MODEL)),
        ],
        out_specs=[
            pl.BlockSpec((1, tl, D_MODEL), lambda s: (prev(s) // nl, prev(s) % nl, 0)),
            pl.BlockSpec((1, CONV_HIST, D_MODEL), lambda s: (cur(s) // nl, 0, 0)),
        ],
        out_shape=[
            jax.ShapeDtypeStruct((bsz, seq, D_MODEL), F32),
            jax.ShapeDtypeStruct((bsz, CONV_HIST, D_MODEL), F32),
        ],
        scratch_shapes=[
            pltpu.VMEM((1, HIST_PAD + tl, D_MODEL), F32),
            pltpu.VMEM((1, tl, D_MODEL), F32),
            pltpu.VMEM((2, tl, D_MODEL), F32),
        ],
        compiler_params=_params(1),
        name="layer0",
    )(h, hist, w["gm0"], w["w_pw1"], w["b_pw1"], w["w_dw"], w["b_dw"], w["ln_g"], w["ln_b"], w["w_pw2"], w["b_pw2"],
      w["gf0"], w["wg0"], w["wu0"], w["wd0"])


def _softplus(x):
    return jnp.maximum(x, 0.0) + jnp.log(1.0 + jnp.exp(-jnp.abs(x)))


def _ssd_front_kernel(h_ref, ch_ref, gm_ref, wz_ref, wxa_ref, wxb_ref, wdt_ref, cw_ref, cb_ref, dtb_ref,
                      z_ref, xbc_ref, dt_ref, nc_ref, hist_ref, *, tb, tl):
    l = pl.program_id(1)
    m = tb * tl

    @pl.when(l == 0)
    def _():
        hist_ref[:, 0:SSD_HIST_PAD - SSD_CONV_HIST, :] = jnp.zeros((tb, SSD_HIST_PAD - SSD_CONV_HIST, CONV_DIM), F32)
        hist_ref[:, SSD_HIST_PAD - SSD_CONV_HIST:, :] = ch_ref[...]

    x = h_ref[...].reshape(m, D_MODEL)
    hn = _rms(x, gm_ref[...]).astype(BF16)
    dt_ref[...] = _softplus(_dot(hn, wdt_ref[...]) + dtb_ref[...]).reshape(tb, tl, LANES)
    row_in_group = lax.broadcasted_iota(jnp.int32, (tb, tl, FRONT_COLS), 1) % SUBLANES

    def conv_block(xn, lo):
        cs = slice(lo, lo + FRONT_COLS)
        xh = jnp.concatenate([hist_ref[:, :, cs], xn], axis=1)
        tap = lambda k: jnp.tile(cw_ref[k, :, cs], (tl // SUBLANES, 1))[None]
        acc = cb_ref[:, cs][None] + xn * tap(SSD_CONV_WIDTH - 1)
        for d in range(1, SSD_CONV_WIDTH):
            rot = _roll_groups_down(xh, d)
            acc = acc + jnp.where(row_in_group < d, rot[:, :tl], rot[:, SUBLANES:]) * tap(SSD_CONV_WIDTH - 1 - d)
        xbc_ref[:, :, cs] = _silu(acc).astype(xbc_ref.dtype)
        hist_ref[:, :, cs] = xn[:, tl - SSD_HIST_PAD:]

    def z_block(zn, lo):
        z_ref[:, :, lo:lo + FRONT_COLS] = zn.astype(z_ref.dtype)

    tasks = []
    z_cols = list(range(0, D_INNER, FRONT_COLS))
    for i, lo in enumerate(range(0, CONV_DIM, FRONT_COLS)):
        w_ref, off = (wxa_ref, lo) if lo < D_INNER else (wxb_ref, lo - D_INNER)
        tasks.append((w_ref, off, conv_block, lo))
        if i % 2 == 1:
            tasks.append((wz_ref, z_cols[i // 2], z_block, z_cols[i // 2]))
    pending = None
    for w_ref, off, finish, lo in tasks:
        res = _dot(hn, w_ref[:, off:off + FRONT_COLS]).reshape(tb, tl, FRONT_COLS)
        if pending is not None:
            pending[0](pending[1], pending[2])
        pending = (finish, res, lo)
    pending[0](pending[1], pending[2])

    @pl.when(l == pl.num_programs(1) - 1)
    def _():
        nc_ref[...] = hist_ref[:, SSD_HIST_PAD - SSD_CONV_HIST:, :]


def _ssd_front(h, ch, gm, w_in, wdt, cw, cb, dtb, *, tb, tl):
    bsz, seq, _ = h.shape
    act_dtype = BF16 if tl % (2 * SUBLANES) == 0 else F32
    w_blk = lambda j: pl.BlockSpec((D_MODEL, D_INNER), lambda *_: (0, j), pipeline_mode=pl.Buffered(1))
    return pl.pallas_call(
        functools.partial(_ssd_front_kernel, tb=tb, tl=tl),
        grid=(bsz // tb, seq // tl),
        in_specs=[
            pl.BlockSpec((tb, tl, D_MODEL), lambda b, l: (b, l, 0)),
            pl.BlockSpec((tb, SSD_CONV_HIST, CONV_DIM), lambda b, l: (0 if ch.shape[0] == 1 else b, 0, 0)),
            _const_spec((1, D_MODEL)),
            w_blk(0),
            w_blk(1),
            w_blk(2),
            _const_spec((D_MODEL, LANES)),
            _const_spec((SSD_CONV_WIDTH, SUBLANES, CONV_DIM)),
            _const_spec((1, CONV_DIM)),
            _const_spec((1, LANES)),
        ],
        out_specs=[
            pl.BlockSpec((tb, tl, D_INNER), lambda b, l: (b, l, 0)),
            pl.BlockSpec((tb, tl, CONV_DIM), lambda b, l: (b, l, 0)),
            pl.BlockSpec((tb, tl, LANES), lambda b, l: (b, l, 0)),
            pl.BlockSpec((tb, SSD_CONV_HIST, CONV_DIM), lambda b, l: (b, 0, 0)),
        ],
        out_shape=[
            jax.ShapeDtypeStruct((bsz, seq, D_INNER), act_dtype),
            jax.ShapeDtypeStruct((bsz, seq, CONV_DIM), act_dtype),
            jax.ShapeDtypeStruct((bsz, seq, LANES), F32),
            jax.ShapeDtypeStruct((bsz, SSD_CONV_HIST, CONV_DIM), F32),
        ],
        scratch_shapes=[pltpu.VMEM((tb, SSD_HIST_PAD, CONV_DIM), F32)],
        compiler_params=_params(2),
        name="ssd_front",
    )(h, ch, gm, w_in, w_in, w_in, wdt, cw, cb, dtb)


def _ssd_front_tm_kernel(x_ref, ch_ref, gm_ref, wz_ref, wxa_ref, wxb_ref, wdt_ref, cw_ref, cb_ref, dtb_ref,
                         z_ref, xbc_ref, dt_ref, nc_ref, *, ls, tb):
    m = ls * tb
    hn = _rms(x_ref[...].reshape(m, D_MODEL), gm_ref[...]).astype(BF16)
    dt_ref[...] = _softplus(_dot(hn, wdt_ref[...]) + dtb_ref[...]).reshape(ls, tb, LANES)
    z_ref[...] = _dot(hn, wz_ref[...]).reshape(ls, tb, D_INNER)
    for lo in range(0, CONV_DIM, FRONT_COLS):
        cs = slice(lo, lo + FRONT_COLS)
        w_ref, off = (wxa_ref, lo) if lo < D_INNER else (wxb_ref, lo - D_INNER)
        xn = _dot(hn, w_ref[:, off:off + FRONT_COLS]).reshape(ls, tb, FRONT_COLS)
        xh = jnp.concatenate([ch_ref[:, :, cs], xn], axis=0)
        acc = cb_ref[:, cs][None]
        for k in range(SSD_CONV_WIDTH):
            acc = acc + xh[k:k + ls] * jnp.tile(cw_ref[k, :, cs], (tb // SUBLANES, 1))[None]
        xbc_ref[:, :, cs] = _silu(acc)
        nc_ref[:, :, cs] = xh[ls:ls + SSD_CONV_HIST]


def _ssd_front_tm(x, ch, w, *, tb):
    ls, bsz, _ = x.shape
    w_in = w["w_in"]
    w_blk = lambda j: pl.BlockSpec((D_MODEL, D_INNER), lambda *_: (0, j), pipeline_mode=pl.Buffered(1))
    seq_blk = lambda rows, width: pl.BlockSpec((rows, tb, width), lambda i: (0, i, 0))
    return pl.pallas_call(
        functools.partial(_ssd_front_tm_kernel, ls=ls, tb=tb),
        grid=(bsz // tb,),
        in_specs=[
            seq_blk(ls, D_MODEL),
            seq_blk(SSD_CONV_HIST, CONV_DIM),
            _const_spec((1, D_MODEL)),
            w_blk(0),
            w_blk(1),
            w_blk(2),
            _const_spec((D_MODEL, LANES)),
            _const_spec((SSD_CONV_WIDTH, SUBLANES, CONV_DIM)),
            _const_spec((1, CONV_DIM)),
            _const_spec((1, LANES)),
        ],
        out_specs=[seq_blk(ls, D_INNER), seq_blk(ls, CONV_DIM), seq_blk(ls, LANES), seq_blk(SSD_CONV_HIST, CONV_DIM)],
        out_shape=[
            jax.ShapeDtypeStruct((ls, bsz, D_INNER), F32),
            jax.ShapeDtypeStruct((ls, bsz, CONV_DIM), F32),
            jax.ShapeDtypeStruct((ls, bsz, LANES), F32),
            jax.ShapeDtypeStruct((SSD_CONV_HIST, bsz, CONV_DIM), F32),
        ],
        compiler_params=_params(1),
        name="ssd_front_tm",
    )(x, ch, w["gm1"], w_in, w_in, w_in, w["w_dt"], w["conv_w"], w["conv_b"], w["dt_bias"])


def _cumsum_rows(a, tri):
    a_hi = a.astype(BF16)
    r1 = a - a_hi.astype(F32)
    a_mid = r1.astype(BF16)
    a_lo = (r1 - a_mid.astype(F32)).astype(BF16)
    return _dot(tri, a_hi) + _dot(tri, a_mid) + _dot(tri, a_lo)


def _head_rows(mat, g, width):
    return jnp.concatenate(
        [jnp.broadcast_to(mat[HEADS_PER_GROUP * g + r:HEADS_PER_GROUP * g + r + 1, :], (HEAD_DIM, width))
         for r in range(HEADS_PER_GROUP)], axis=0)


def _scan_fns(xbc_ref, z_ref, st_ref, vn_ref, alog_ref, dske_ref, ng_ref, lc):
    row = lax.broadcasted_iota(jnp.int32, (lc, lc), 0)
    col = lax.broadcasted_iota(jnp.int32, (lc, lc), 1)
    causal = row >= col
    tri = jnp.where(causal, 1.0, 0.0).astype(BF16)
    lane_head = lax.broadcasted_iota(jnp.int32, (lc, GROUP_DIM), 1) // HEAD_DIM
    lane_lo = lax.broadcasted_iota(jnp.int32, (lc, LANES), 1) < HEAD_DIM
    neg_a = -jnp.exp(alog_ref[...])

    def group_dots(i, r0, g):
        b_off = D_INNER + g * D_STATE
        c_off = D_INNER + N_GROUPS * D_STATE + g * D_STATE
        bg = xbc_ref[i, r0:r0 + lc, b_off:b_off + D_STATE].astype(BF16)
        cg = xbc_ref[i, r0:r0 + lc, c_off:c_off + D_STATE].astype(BF16)
        cb = lax.dot_general(cg, bg, _NT, preferred_element_type=F32)
        hg = st_ref[i, g]
        yoff = lax.dot_general(cg, hg.astype(BF16), _NT, preferred_element_type=F32)
        return bg, cb, hg, yoff

    def decays(dt):
        acs = _cumsum_rows(dt * neg_a, tri)
        acs_t = acs.T
        dt_t = dt.T
        last_t = jnp.broadcast_to(acs_t[:, lc - 1:lc], (LANES, lc))
        c_t = acs_t - jnp.log(dt_t)
        coef_t = jnp.exp(last_t - acs_t) * dt_t
        cd = jnp.broadcast_to(jnp.exp(acs_t[:, lc - 1:lc]), (LANES, D_STATE))
        return acs, c_t, coef_t, cd

    def group_tail(i, r0, g, dots, dec):
        bg, cb, hg, yoff = dots
        acs, c_t, coef_t, cd = dec
        lo, hi = g * GROUP_DIM, (g + 1) * GROUP_DIM
        xg = xbc_ref[i, r0:r0 + lc, lo:hi].astype(F32)
        ws, acs_b = [], []
        for r in range(HEADS_PER_GROUP):
            h = HEADS_PER_GROUP * g + r
            ab = jnp.broadcast_to(acs[:, h:h + 1], (lc, LANES))
            acs_b.append(ab)
            ws.append((cb * jnp.exp(jnp.where(causal, ab - c_t[h:h + 1, :], -jnp.inf))).astype(BF16))
        x_bd = jnp.concatenate([jnp.where(lane_head == r, xg, 0.0).astype(BF16) for r in range(HEADS_PER_GROUP)],
                               axis=0)
        yd = _dot(jnp.concatenate(ws, axis=1), x_bd)
        e_b = jnp.exp(jnp.concatenate([jnp.where(lane_lo, acs_b[0], acs_b[1]),
                                       jnp.where(lane_lo, acs_b[2], acs_b[3])], axis=1))
        y = yd + e_b * yoff + dske_ref[:, lo:hi] * xg

        xcoef_t = (xg.T * _head_rows(coef_t, g, lc)).astype(BF16)
        st_ref[i, g] = _head_rows(cd, g, D_STATE) * hg + _dot(xcoef_t, bg)

        v = y * _silu(z_ref[i, r0:r0 + lc, lo:hi].astype(F32))
        vn = v * lax.rsqrt(jnp.mean(v * v, axis=-1, keepdims=True) + EPS) * ng_ref[:, lo:hi]
        vn_ref[i, r0:r0 + lc, lo:hi] = vn.astype(BF16)

    return group_dots, decays, group_tail


def _ssd_scan_kernel(xbc_ref, dt_ref, dtn_ref, z_ref, h_ref, h0_ref, alog_ref, dske_ref, ng_ref, wo_ref, o_ref, st_ref,
                     vn_ref, dec_ref, *, lc, n_chunks):
    c = pl.program_id(1)
    first_slot = 0 if n_chunks % 2 == 0 else c % 2
    group_dots, decays, group_tail = _scan_fns(xbc_ref, z_ref, st_ref, vn_ref, alog_ref, dske_ref, ng_ref, lc)

    @pl.when(c == 0)
    def _():
        st_ref[...] = h0_ref[...]
        for k, v in enumerate(decays(dt_ref[0, 0:lc, :])):
            dec_ref[0, k] = v

    half = D_INNER // 2
    for ci in range(n_chunks):
        r0 = ci * lc
        slot = (first_slot + ci) % 2
        dots = group_dots(0, r0, 0)
        dec = tuple(dec_ref[slot, k] for k in range(4))
        for g in range(N_GROUPS):
            nxt = group_dots(0, r0, g + 1) if g + 1 < N_GROUPS else None
            group_tail(0, r0, g, dots, dec)
            dots = nxt
            if g == 1:
                next_dt = dt_ref[0, r0 + lc:r0 + 2 * lc, :] if ci + 1 < n_chunks else dtn_ref[0]
                for k, v in enumerate(decays(next_dt)):
                    dec_ref[1 - slot, k] = v
            if g == N_GROUPS // 2 - 1:
                o_ref[0, r0:r0 + lc, :] = h_ref[0, r0:r0 + lc, :] + _dot(vn_ref[0, r0:r0 + lc, :half], wo_ref[:half, :])
        o_ref[0, r0:r0 + lc, :] = o_ref[0, r0:r0 + lc, :] + _dot(vn_ref[0, r0:r0 + lc, half:], wo_ref[half:, :])


def _ssd_scan(xbc, dt, z, h, h0, alog, dske, ng, wo, *, lc):
    bsz, seq, _ = xbc.shape
    assert lc == LANES
    nb = 1
    n_chunks = SCAN_CHUNKS if seq % (SCAN_CHUNKS * lc) == 0 else 1
    rows = n_chunks * lc
    last = seq // lc - 1
    shared_h0 = h0.shape[0] == 1
    return pl.pallas_call(
        functools.partial(_ssd_scan_kernel, lc=lc, n_chunks=n_chunks),
        grid=(bsz // nb, seq // rows),
        in_specs=[
            pl.BlockSpec((nb, rows, CONV_DIM), lambda b, c: (b, c, 0)),
            pl.BlockSpec((nb, rows, LANES), lambda b, c: (b, c, 0)),
            pl.BlockSpec((nb, lc, LANES), lambda b, c: (b, jnp.minimum((c + 1) * n_chunks, last), 0)),
            pl.BlockSpec((nb, rows, D_INNER), lambda b, c: (b, c, 0)),
            pl.BlockSpec((nb, rows, D_MODEL), lambda b, c: (b, c, 0)),
            pl.BlockSpec((nb, N_GROUPS, GROUP_DIM, D_STATE), lambda b, c: (0 if shared_h0 else b, 0, 0, 0)),
            _const_spec((1, LANES)),
            _const_spec((1, D_INNER)),
            _const_spec((1, D_INNER)),
            _const_spec((D_INNER, D_MODEL)),
        ],
        out_specs=[
            pl.BlockSpec((nb, rows, D_MODEL), lambda b, c: (b, c, 0)),
            pl.BlockSpec((nb, N_GROUPS, GROUP_DIM, D_STATE), lambda b, c: (b, 0, 0, 0)),
        ],
        out_shape=[
            jax.ShapeDtypeStruct((bsz, seq, D_MODEL), F32),
            jax.ShapeDtypeStruct((bsz, N_GROUPS, GROUP_DIM, D_STATE), F32),
        ],
        scratch_shapes=[pltpu.VMEM((nb, rows, D_INNER), BF16), pltpu.VMEM((2, 4, LANES, LANES), F32)],
        compiler_params=_params(2),
        name="ssd_scan",
    )(xbc, dt, dt, z, h, h0, alog, dske, ng, wo)


def _ssd_step_kernel(x_ref, b_ref, c_ref, dt_ref, st_ref, alog_ref, dsk_ref, y_ref, so_ref, *, nb, ls, gs):
    rows = nb * ls
    row = lax.broadcasted_iota(jnp.int32, (rows, rows), 0)
    col = lax.broadcasted_iota(jnp.int32, (rows, rows), 1)
    same = (row % nb) == (col % nb)
    causal = jnp.logical_and(same, row >= col)
    seq_of_col = lax.broadcasted_iota(jnp.int32, (GROUP_DIM, rows), 1) % nb
    hi = lax.Precision.HIGHEST
    for j in range(gs):
        g = pl.program_id(1) * gs + j
        shift = (LANES - HEADS_PER_GROUP * g) % LANES
        dt = pltpu.roll(dt_ref[...].reshape(rows, LANES), shift, axis=1)
        alog = pltpu.roll(alog_ref[...], shift, axis=1)
        dsk = pltpu.roll(dsk_ref[...], shift, axis=1)
        a = dt * (-jnp.exp(alog))
        acs = jnp.dot(jnp.where(causal, 1.0, 0.0), a, preferred_element_type=F32, precision=hi)
        tot = jnp.dot(jnp.where(same, 1.0, 0.0), a, preferred_element_type=F32, precision=hi)
        acs_t = acs.T
        coef = jnp.exp(tot - acs) * dt
        eacs = jnp.exp(acs)
        etot = jnp.exp(tot)

        bg = b_ref[:, :, j * D_STATE:(j + 1) * D_STATE].reshape(rows, D_STATE).astype(BF16)
        cg = c_ref[:, :, j * D_STATE:(j + 1) * D_STATE].reshape(rows, D_STATE).astype(BF16)
        xg = x_ref[:, :, j * GROUP_DIM:(j + 1) * GROUP_DIM].reshape(rows, GROUP_DIM)
        cb = lax.dot_general(cg, bg, _NT, preferred_element_type=F32)

        yoff_t = jnp.zeros((GROUP_DIM, rows), F32)
        for b in range(nb):
            yb = lax.dot_general(st_ref[b, j].astype(BF16), cg, _NT, preferred_element_type=F32)
            yoff_t = jnp.where(seq_of_col == b, yb, yoff_t)
        yoff = yoff_t.T

        ys, xcs = [], []
        for r in range(HEADS_PER_GROUP):
            decay = jnp.exp(jnp.where(causal, acs[:, r:r + 1] - acs_t[r:r + 1, :], -jnp.inf))
            w = (cb * decay).astype(BF16)
            xr = xg[:, r * HEAD_DIM:(r + 1) * HEAD_DIM]
            yd = _dot(w, (xr * dt[:, r:r + 1]).astype(BF16))
            ys.append(yd + eacs[:, r:r + 1] * yoff[:, r * HEAD_DIM:(r + 1) * HEAD_DIM] + dsk[:, r:r + 1] * xr)
            xcs.append(xr * coef[:, r:r + 1])
        y_ref[:, :, j * GROUP_DIM:(j + 1) * GROUP_DIM] = jnp.concatenate(ys, axis=1).reshape(ls, nb, GROUP_DIM)
        xcoef_t = jnp.concatenate(xcs, axis=1).T

        for b in range(nb):
            upd = _dot(jnp.where(seq_of_col == b, xcoef_t, 0.0).astype(BF16), bg)
            scale = jnp.concatenate(
                [jnp.broadcast_to(etot[b:b + 1, r:r + 1], (HEAD_DIM, D_STATE)) for r in range(HEADS_PER_GROUP)],
                axis=0)
            so_ref[b, j] = scale * st_ref[b, j] + upd


def _ssd_step(xbc, dt, st, alog, dsk, *, nb, ls):
    n_seq = st.shape[0]
    gs = STEP_GROUPS
    b_blk0 = D_INNER // (gs * D_STATE)
    c_blk0 = b_blk0 + N_GROUPS // gs
    return pl.pallas_call(
        functools.partial(_ssd_step_kernel, nb=nb, ls=ls, gs=gs),
        grid=(n_seq // nb, N_GROUPS // gs),
        in_specs=[
            pl.BlockSpec((ls, nb, gs * GROUP_DIM), lambda i, g: (0, i, g)),
            pl.BlockSpec((ls, nb, gs * D_STATE), lambda i, g: (0, i, b_blk0 + g)),
            pl.BlockSpec((ls, nb, gs * D_STATE), lambda i, g: (0, i, c_blk0 + g)),
            pl.BlockSpec((ls, nb, LANES), lambda i, g: (0, i, 0)),
            pl.BlockSpec((nb, gs, GROUP_DIM, D_STATE), lambda i, g: (i, g, 0, 0)),
            _const_spec((1, LANES)),
            _const_spec((1, LANES)),
        ],
        out_specs=[
            pl.BlockSpec((ls, nb, gs * GROUP_DIM), lambda i, g: (0, i, g)),
            pl.BlockSpec((nb, gs, GROUP_DIM, D_STATE), lambda i, g: (i, g, 0, 0)),
        ],
        out_shape=[
            jax.ShapeDtypeStruct((ls, n_seq, D_INNER), F32),
            jax.ShapeDtypeStruct((n_seq, N_GROUPS, GROUP_DIM, D_STATE), F32),
        ],
        compiler_params=_params(2),
        name="ssd_step",
    )(xbc, xbc, xbc, dt, st, alog, dsk)


def _ssd_post_kernel(y_ref, z_ref, h_ref, ng_ref, wo_ref, o_ref):
    acc = h_ref[...]
    for g in range(N_GROUPS):
        lo, hi = g * GROUP_DIM, (g + 1) * GROUP_DIM
        v = y_ref[:, lo:hi] * _silu(z_ref[:, lo:hi])
        vn = v * lax.rsqrt(jnp.mean(v * v, axis=-1, keepdims=True) + EPS) * ng_ref[:, lo:hi]
        acc = acc + _dot(vn.astype(BF16), wo_ref[lo:hi, :])
    o_ref[...] = acc


def _ssd_post(y, z, h, ng, wo, *, tm):
    t = y.shape[0]
    return pl.pallas_call(
        _ssd_post_kernel,
        grid=(t // tm,),
        in_specs=[
            pl.BlockSpec((tm, D_INNER), lambda i: (i, 0)),
            pl.BlockSpec((tm, D_INNER), lambda i: (i, 0)),
            pl.BlockSpec((tm, D_MODEL), lambda i: (i, 0)),
            _const_spec((1, D_INNER)),
            _const_spec((D_INNER, D_MODEL)),
        ],
        out_specs=pl.BlockSpec((tm, D_MODEL), lambda i: (i, 0)),
        out_shape=jax.ShapeDtypeStruct((t, D_MODEL), F32),
        compiler_params=_params(1),
        name="ssd_post",
    )(y, z, h, ng, wo)


def _trunk(h, cc_hist, sc_hist, ssm, w, *, tb, tl, tm, tl_front=None):
    bsz, seq, _ = h.shape
    t = bsz * seq
    if tb == 1 and tl % 64 == 0:
        h2, new_cc = _layer0(h, cc_hist, w, tl=tl)
        h2 = h2.reshape(t, D_MODEL)
    else:
        h1, new_cc = _conformer(h, cc_hist, w["gm0"], w["w_pw1"], w["b_pw1"], w["w_dw"], w["b_dw"], w["ln_g"],
                                w["ln_b"], w["w_pw2"], w["b_pw2"], tb=tb, tl=tl)
        h2 = _ffn(h1.reshape(t, D_MODEL), w["gf0"], w["wg0"], w["wu0"], w["wd0"], w["g_final"], tm=tm, final=False)
    z, xbc, dt, new_sc = _ssd_front(h2.reshape(bsz, seq, D_MODEL), sc_hist, w["gm1"], w["w_in"], w["w_dt"],
                                    w["conv_w"], w["conv_b"], w["dt_bias"], tb=tb, tl=tl_front or tl)
    pad = (-seq) % LANES
    rows = lambda v: jnp.pad(v, ((0, 0), (0, pad), (0, 0))) if pad else v
    h3, new_ss = _ssd_scan(rows(xbc), rows(dt), rows(z), rows(h2.reshape(bsz, seq, D_MODEL)), ssm,
                           w["a_log"], w["d_skip_lanes"], w["norm_g"], w["w_out"], lc=LANES)
    h3 = h3[:, :seq].reshape(t, D_MODEL)
    out = _ffn(h3, w["gf1"], w["wg1"], w["wu1"], w["wd1"], w["g_final"], tm=tm, final=True)
    return out.reshape(bsz, seq, D_MODEL), new_cc, new_sc, new_ss


def _trunk_short(x, cc_hist, sc_hist, ssm, w, *, tb, tm):
    ls, bsz, _ = x.shape
    t = ls * bsz
    h1, new_cc = _conformer_tm(x, cc_hist, w, tb=tb)
    h2 = _ffn(h1.reshape(t, D_MODEL), w["gf0"], w["wg0"], w["wu0"], w["wd0"], w["g_final"], tm=tm, final=False)
    z, xbc, dt, new_sc = _ssd_front_tm(h2.reshape(ls, bsz, D_MODEL), sc_hist, w, tb=tb)
    y, new_ss = _ssd_step(xbc, dt, ssm, w["a_log"], w["d_skip"], nb=LANES // ls, ls=ls)
    h3 = _ssd_post(y.reshape(t, D_INNER), z.reshape(t, D_INNER), h2, w["norm_g"], w["w_out"], tm=tm)
    out = _ffn(h3, w["gf1"], w["wg1"], w["wu1"], w["wd1"], w["g_final"], tm=tm, final=True)
    return out.reshape(ls, bsz, D_MODEL), new_cc, new_sc, new_ss


def _pad_lanes(v):
    return jnp.pad(v.astype(F32), (0, LANES - v.shape[0])).reshape(1, LANES)


def kernel(x_prompt, x_sample, cache_conv, state_ssd_conv, state_ssm, meta_tokens, norm_mix, norm_ffn, norm_final, cf_w_pw1, cf_b_pw1, cf_w_dw, cf_b_dw, cf_ln_g, cf_ln_b, cf_w_pw2, cf_b_pw2, ssd_w_in, ssd_conv_w, ssd_conv_b, ssd_dt_bias, ssd_a_log, ssd_d, ssd_norm_g, ssd_w_out, ffn_w_gate, ffn_w_up, ffn_w_down):
    row = lambda v: v.astype(F32).reshape(1, -1)
    w_in = ssd_w_in[0]
    w = {
        "gm0": row(norm_mix[0]), "gm1": row(norm_mix[1]),
        "gf0": row(norm_ffn[0]), "gf1": row(norm_ffn[1]), "g_final": row(norm_final),
        "w_pw1": cf_w_pw1[0].astype(BF16), "b_pw1": row(cf_b_pw1[0]),
        "w_dw": jnp.broadcast_to(cf_w_dw[0][:, None, :], (CONV_WIDTH, SUBLANES, D_MODEL)),
        "b_dw": row(cf_b_dw[0]), "ln_g": row(cf_ln_g[0]), "ln_b": row(cf_ln_b[0]),
        "w_pw2": cf_w_pw2[0].astype(BF16), "b_pw2": row(cf_b_pw2[0]),
        "w_in": w_in.astype(BF16),
        "w_dt": jnp.pad(w_in[:, D_INNER + CONV_DIM:], ((0, 0), (0, LANES - N_HEADS))).astype(BF16),
        "conv_w": jnp.broadcast_to(ssd_conv_w[0][:, None, :], (SSD_CONV_WIDTH, SUBLANES, CONV_DIM)),
        "conv_b": row(ssd_conv_b[0]),
        "dt_bias": _pad_lanes(ssd_dt_bias[0]), "a_log": _pad_lanes(ssd_a_log[0]), "d_skip": _pad_lanes(ssd_d[0]),
        "d_skip_lanes": jnp.repeat(ssd_d[0].astype(F32), HEAD_DIM).reshape(1, D_INNER),
        "norm_g": row(ssd_norm_g[0]), "w_out": ssd_w_out[0].astype(BF16),
        "wg0": ffn_w_gate[0].astype(BF16), "wu0": ffn_w_up[0].astype(BF16), "wd0": ffn_w_down[0].astype(BF16),
        "wg1": ffn_w_gate[1].astype(BF16), "wu1": ffn_w_up[1].astype(BF16), "wd1": ffn_w_down[1].astype(BF16),
    }
    bp, seq, _ = x_prompt.shape
    bs = x_sample.shape[0]

    _, cc_m, sc_m, ss_m = _trunk(
        meta_tokens.astype(F32)[None], jnp.zeros((1, CONV_HIST, D_MODEL), F32),
        jnp.zeros((1, SSD_CONV_HIST, CONV_DIM), F32), jnp.zeros((1, N_GROUPS, GROUP_DIM, D_STATE), F32), w,
        tb=1, tl=N_META, tm=N_META)

    y_prompt, cc_p, sc_p, ss_p = _trunk(
        x_prompt, cc_m, sc_m,
        ss_m, w, tb=1, tl=LAYER0_ROWS, tl_front=FRONT_ROWS, tm=FFN_ROWS)

    swap = lambda v: jnp.transpose(v, (1, 0, 2))
    y_s, cc_s, sc_s, ss_s = _trunk_short(
        swap(x_sample), swap(cache_conv[0]), swap(state_ssd_conv[0]),
        state_ssm[0].reshape(bs, N_GROUPS, GROUP_DIM, D_STATE), w, tb=SHORT_SEQS, tm=FFN_ROWS)

    unpack_ss = lambda v: v.reshape(1, v.shape[0], N_HEADS, HEAD_DIM, D_STATE)
    return (y_prompt, swap(y_s), cc_p[None], swap(cc_s)[None], sc_p[None], swap(sc_s)[None],
            unpack_ss(ss_p), unpack_ss(ss_s))
```
